```python
import math
import jax, jax.numpy as jnp
from jax import lax
import numpy as np

D_MODEL = 1024
BATCH = 8
SEQ = 2048
DEPTH = 4
DEC_BATCH = 2
DEC_SEQ = 8192
PAST_LEN = 128

GRID_W = 64
Q_BLOCK = 128
EPS = 1e-6
DA_HEADS = 8
DA_HEAD_DIM = 64
DA_V_DIM = 2 * DA_HEAD_DIM
NA_HEADS = 16
NA_HEAD_DIM = 64
NA_ROWS_MAX = 8
NA_COLS = 16
GQ_HEADS = 8
GQ_KV_HEADS = 2
GQ_HEAD_DIM = 128
ROPE_THETA = 10000.0
D_FF = 4 * D_MODEL
PLE_DIM = 256
N_BRANCH = 3

DA_Q = DA_HEADS * 2 * DA_HEAD_DIM
DA_K = DA_HEADS * 2 * DA_HEAD_DIM
DA_V = DA_HEADS * DA_V_DIM
NA_W = NA_HEADS * NA_HEAD_DIM
GQ_Q = GQ_HEADS * GQ_HEAD_DIM
GQ_KV = GQ_KV_HEADS * GQ_HEAD_DIM
GATE_W = N_BRANCH * D_MODEL
IN_SIZES = (DA_Q, DA_K, DA_V, NA_W, NA_W, NA_W, GQ_Q, GQ_KV, GQ_KV, GATE_W)
IN_W = DA_Q + DA_K + DA_V + 3 * NA_W + GQ_Q + 2 * GQ_KV + GATE_W

kernel_name = "hybrid_gated_parallel_encoder"


def rmsnorm(x, g):
    xf = x.astype(jnp.float32)
    y = xf * lax.rsqrt(jnp.mean(xf * xf, axis=-1, keepdims=True) + EPS)
    return (y * g.astype(jnp.float32)).astype(x.dtype)


def split_cols(z):
    outs = []
    off = 0
    for n in IN_SIZES:
        outs.append(z[..., off:off + n])
        off += n
    return outs


def diff_attention(q, k, v, lam):
    B, S, H = q.shape[0], q.shape[1], q.shape[2]
    nb = S // Q_BLOCK
    scale = DA_HEAD_DIM ** -0.5
    slopes = jnp.exp2(-8.0 * jnp.arange(1, H + 1, dtype=jnp.float32) / H)
    pos = jnp.arange(S)
    qb = q.reshape(B, nb, Q_BLOCK, H, 2, DA_HEAD_DIM).transpose(1, 0, 2, 3, 4, 5)

    def block(args):
        qblk, start = args
        s = jnp.einsum('bqhmd,bkhmd->bmhqk', qblk, k).astype(jnp.float32) * scale
        qpos = start + jnp.arange(Q_BLOCK)
        dist = jnp.abs(qpos[:, None] - pos[None, :]).astype(jnp.float32)
        s = s - slopes[:, None, None] * dist
        p = jax.nn.softmax(s, axis=-1)
        a = p[:, 0] - lam * p[:, 1]
        return jnp.einsum('bhqk,bkhe->bqhe', a.astype(v.dtype), v)

    out = lax.map(block, (qb, jnp.arange(nb) * Q_BLOCK))
    return out.transpose(1, 0, 2, 3, 4).reshape(B, S, H, DA_V_DIM)


def neighbourhood_attention(q, k, v, rpb):
    B, S, H, d = q.shape
    rows = S // GRID_W
    kr = min(NA_ROWS_MAX, rows)
    scale = NA_HEAD_DIM ** -0.5
    r = jnp.arange(rows)
    row_start = jnp.clip(r - kr // 2, 0, rows - kr)
    c = jnp.arange(GRID_W)
    col_start = jnp.clip(c - NA_COLS // 2, 0, GRID_W - NA_COLS)
    in_win = (c[None, :] >= col_start[:, None]) & (c[None, :] < col_start[:, None] + NA_COLS)
    dc = jnp.clip(c[None, :] - c[:, None], -(NA_COLS - 1), NA_COLS - 1) + (NA_COLS - 1)
    qg = q.reshape(B, rows, GRID_W, H, d).transpose(1, 0, 2, 3, 4)
    kg = k.reshape(B, rows, GRID_W, H, d)
    vg = v.reshape(B, rows, GRID_W, H, d)

    def row_block(args):
        q_row, r0, start = args
        k_rows = lax.dynamic_slice_in_dim(kg, start, kr, axis=1)
        v_rows = lax.dynamic_slice_in_dim(vg, start, kr, axis=1)
        dr = start + jnp.arange(kr) - r0 + (NA_ROWS_MAX - 1)
        bias = rpb[:, dr[:, None, None], dc[None, :, :]].transpose(0, 2, 1, 3)
        s = jnp.einsum('bqhd,bikhd->bhqik', q_row, k_rows).astype(jnp.float32) * scale
        s = s + bias[None].astype(jnp.float32)
        s = jnp.where(in_win[:, None, :], s, -jnp.inf)
        p = jax.nn.softmax(s.reshape(B, H, GRID_W, kr * GRID_W), axis=-1).reshape(s.shape)
        return jnp.einsum('bhqik,bikhd->bqhd', p.astype(v.dtype), v_rows)

    out = lax.map(row_block, (qg, r, row_start))
    return out.transpose(1, 0, 2, 3, 4).reshape(B, S, H, d)


def axial_rope_tables(S):
    t = jnp.arange(S)
    row = (t // GRID_W).astype(jnp.float32)
    col = (t % GRID_W).astype(jnp.float32)
    half = GQ_HEAD_DIM // 2
    freqs = ROPE_THETA ** (-jnp.arange(0, half, 2, dtype=jnp.float32) / half)
    ang = jnp.concatenate([row[:, None] * freqs, col[:, None] * freqs], axis=-1)
    return jnp.cos(ang), jnp.sin(ang)


def apply_rope(x, cos, sin):
    x2 = x.astype(jnp.float32).reshape(x.shape[:-1] + (x.shape[-1] // 2, 2))
    x0, x1 = x2[..., 0], x2[..., 1]
    c = cos[None, :, None, :]
    s = sin[None, :, None, :]
    out = jnp.stack([x0 * c - x1 * s, x0 * s + x1 * c], axis=-1)
    return out.reshape(x.shape).astype(x.dtype)


def gqa_attention(q, k, v):
    B, S = q.shape[0], q.shape[1]
    G = GQ_HEADS // GQ_KV_HEADS
    nb = S // Q_BLOCK
    scale = GQ_HEAD_DIM ** -0.5
    qb = q.reshape(B, nb, Q_BLOCK, GQ_KV_HEADS, G, GQ_HEAD_DIM).transpose(1, 0, 2, 3, 4, 5)

    def block(qblk):
        s = jnp.einsum('bqngd,bknd->bngqk', qblk, k).astype(jnp.float32) * scale
        p = jax.nn.softmax(s, axis=-1)
        return jnp.einsum('bngqk,bknd->bqngd', p.astype(v.dtype), v)

    out = lax.map(block, qb)
    return out.transpose(1, 0, 2, 3, 4, 5).reshape(B, S, GQ_Q)


def run_trunk(x, ple, w_in, da_lambda, da_norm, na_rpb, gq_q_norm, gq_k_norm,
              w_br_a, w_br_b, w_br_c, w_o, g_pre_mix, g_post_mix, g_pre_mlp, g_post_mlp,
              w_up, w_down, w_ple, w_ple_gate, g_ple):
    B, S = x.shape[0], x.shape[1]
    cos, sin = axial_rope_tables(S)
    h = x
    for i in range(DEPTH):
        lambda_init = 0.8 - 0.6 * math.exp(-0.3 * i)
        u = rmsnorm(h, g_pre_mix[i])
        z = u @ w_in[i]
        qa, ka, va, qn, kn, vn, qc, kc, vc, gz = split_cols(z)
        lq1, lk1, lq2, lk2 = (da_lambda[i, j].astype(jnp.float32) for j in range(4))
        lam = jnp.exp(jnp.sum(lq1 * lk1)) - jnp.exp(jnp.sum(lq2 * lk2)) + lambda_init
        oa = diff_attention(qa.reshape(B, S, DA_HEADS, 2, DA_HEAD_DIM),
                            ka.reshape(B, S, DA_HEADS, 2, DA_HEAD_DIM),
                            va.reshape(B, S, DA_HEADS, DA_V_DIM), lam)
        oa = (rmsnorm(oa, da_norm[i]) * (1.0 - lambda_init)).reshape(B, S, DA_V)
        ob = neighbourhood_attention(qn.reshape(B, S, NA_HEADS, NA_HEAD_DIM),
                                     kn.reshape(B, S, NA_HEADS, NA_HEAD_DIM),
                                     vn.reshape(B, S, NA_HEADS, NA_HEAD_DIM),
                                     na_rpb[i]).reshape(B, S, NA_W)
        qc = apply_rope(rmsnorm(qc.reshape(B, S, GQ_HEADS, GQ_HEAD_DIM), gq_q_norm[i]), cos, sin)
        kc = apply_rope(rmsnorm(kc.reshape(B, S, GQ_KV_HEADS, GQ_HEAD_DIM), gq_k_norm[i]), cos, sin)
        oc = gqa_attention(qc, kc, vc.reshape(B, S, GQ_KV_HEADS, GQ_HEAD_DIM))
        gates = jax.nn.sigmoid(gz.reshape(B, S, N_BRANCH, D_MODEL))
        merged = (gates[:, :, 0] * (oa @ w_br_a[i]) + gates[:, :, 1] * (ob @ w_br_b[i])
                  + gates[:, :, 2] * (oc @ w_br_c[i]))
        h = h + rmsnorm(merged @ w_o[i], g_post_mix[i])
        u = rmsnorm(h, g_pre_mlp[i])
        f = jnp.square(jax.nn.relu(u @ w_up[i])) @ w_down[i]
        h = h + rmsnorm(f, g_post_mlp[i])
        e = ple[i] @ w_ple[i]
        gate = jax.nn.sigmoid(h @ w_ple_gate[i])
        h = h + rmsnorm(e * gate, g_ple[i])
    return h


def setup_inputs(seed: int = 0) -> dict:
    key = jax.random.key(seed)
    ks = jax.random.split(key, 24)
    f32 = jnp.float32

    def nrm(k, shape, scale):
        return jax.random.normal(k, shape, f32) * scale

    def gain(k, shape):
        return 1.0 + 0.05 * jax.random.normal(k, shape, f32)

    return {
        "x_prompt": nrm(ks[0], (BATCH, SEQ, D_MODEL), 1.0),
        "x_sample": nrm(ks[1], (DEC_BATCH, DEC_SEQ, D_MODEL), 1.0),
        "p_prompt": nrm(ks[2], (DEPTH, BATCH, SEQ, PLE_DIM), 1.0),
        "p_sample": nrm(ks[3], (DEPTH, DEC_BATCH, DEC_SEQ, PLE_DIM), 1.0),
        "w_in": nrm(ks[4], (DEPTH, D_MODEL, IN_W), D_MODEL ** -0.5),
        "da_lambda": nrm(ks[5], (DEPTH, 4, DA_HEAD_DIM), 0.1),
        "da_norm": gain(ks[6], (DEPTH, DA_V_DIM)),
        "na_rpb": nrm(ks[7], (DEPTH, NA_HEADS, 2 * NA_ROWS_MAX - 1, 2 * NA_COLS - 1), 0.1),
        "gq_q_norm": gain(ks[8], (DEPTH, GQ_HEAD_DIM)),
        "gq_k_norm": gain(ks[9], (DEPTH, GQ_HEAD_DIM)),
        "w_br_a": nrm(ks[10], (DEPTH, DA_V, D_MODEL), DA_V ** -0.5),
        "w_br_b": nrm(ks[11], (DEPTH, NA_W, D_MODEL), NA_W ** -0.5),
        "w_br_c": nrm(ks[12], (DEPTH, GQ_Q, D_MODEL), GQ_Q ** -0.5),
        "w_o": nrm(ks[13], (DEPTH, D_MODEL, D_MODEL), D_MODEL ** -0.5),
        "g_pre_mix": gain(ks[14], (DEPTH, D_MODEL)),
        "g_post_mix": gain(ks[15], (DEPTH, D_MODEL)),
        "g_pre_mlp": gain(ks[16], (DEPTH, D_MODEL)),
        "g_post_mlp": gain(ks[17], (DEPTH, D_MODEL)),
        "w_up": nrm(ks[18], (DEPTH, D_MODEL, D_FF), D_MODEL ** -0.5),
        "w_down": nrm(ks[19], (DEPTH, D_FF, D_MODEL), D_FF ** -0.5),
        "w_ple": nrm(ks[20], (DEPTH, PLE_DIM, D_MODEL), PLE_DIM ** -0.5),
        "w_ple_gate": nrm(ks[21], (DEPTH, D_MODEL, D_MODEL), D_MODEL ** -0.5),
        "g_ple": gain(ks[22], (DEPTH, D_MODEL)),
    }


def reference(x_prompt, x_sample, p_prompt, p_sample, w_in, da_lambda, da_norm, na_rpb,
              gq_q_norm, gq_k_norm, w_br_a, w_br_b, w_br_c, w_o, g_pre_mix, g_post_mix,
              g_pre_mlp, g_post_mlp, w_up, w_down, w_ple, w_ple_gate, g_ple):
    y_prompt = run_trunk(x_prompt, p_prompt, w_in, da_lambda, da_norm, na_rpb, gq_q_norm, gq_k_norm,
                         w_br_a, w_br_b, w_br_c, w_o, g_pre_mix, g_post_mix, g_pre_mlp, g_post_mlp,
                         w_up, w_down, w_ple, w_ple_gate, g_ple)
    y_sample = run_trunk(x_sample, p_sample, w_in, da_lambda, da_norm, na_rpb, gq_q_norm, gq_k_norm,
                         w_br_a, w_br_b, w_br_c, w_o, g_pre_mix, g_post_mix, g_pre_mlp, g_post_mlp,
                         w_up, w_down, w_ple, w_ple_gate, g_ple)
    return (y_prompt, y_sample)
```

```python
import functools
import math

import jax
import jax.numpy as jnp
from jax import lax
from jax.experimental import pallas as pl
from jax.experimental.pallas import tpu as pltpu

F32 = jnp.float32
BF16 = jnp.bfloat16

D_MODEL = 1024
DEPTH = 4
GRID_W = 64
EPS = 1e-6
DA_HEADS = 8
DA_HEAD_DIM = 64
NA_HEADS = 16
NA_HEAD_DIM = 64
NA_ROWS_MAX = 8
NA_COLS = 16
GQ_HEADS = 8
GQ_KV_HEADS = 2
GQ_HEAD_DIM = 128
ROPE_THETA = 10000.0
D_FF = 4 * D_MODEL
PLE_DIM = 256
IN_W = 10752

LANES = 128
COL_QA, COL_KA, COL_VA = 0, 8, 16
COL_QN, COL_KN, COL_VN = 24, 32, 40
COL_QC, COL_KC, COL_VC = 48, 56, 58
COL_GZ = 60
NEG_BIG = -1e30

VMEM_LIMIT = 48 * 1024 * 1024


def _params(*sem):
    return pltpu.CompilerParams(dimension_semantics=sem, vmem_limit_bytes=VMEM_LIMIT)


def _rms(x, g):
    return x * lax.rsqrt(jnp.mean(x * x, axis=-1, keepdims=True) + EPS) * g


def _norm_matmul_kernel(x_ref, g_ref, w_ref, o_ref, u_ref):
    @pl.when(pl.program_id(1) == 0)
    def _():
        u_ref[...] = _rms(x_ref[...], g_ref[...]).astype(BF16)

    o_ref[...] = jnp.dot(u_ref[...], w_ref[...], preferred_element_type=F32).astype(o_ref.dtype)


def _norm_matmul(h, g, w, tm=1024, tn=1536):
    n, d = h.shape
    width = w.shape[1]
    return pl.pallas_call(
        _norm_matmul_kernel,
        grid=(n // tm, width // tn),
        in_specs=[
            pl.BlockSpec((tm, d), lambda i, j: (i, 0)),
            pl.BlockSpec((1, d), lambda i, j: (0, 0)),
            pl.BlockSpec((d, tn), lambda i, j: (0, j)),
        ],
        out_specs=pl.BlockSpec((tm, tn), lambda i, j: (i, j)),
        out_shape=jax.ShapeDtypeStruct((n, width), BF16),
        scratch_shapes=[pltpu.VMEM((tm, d), BF16)],
        compiler_params=_params("parallel", "arbitrary"),
        name="norm_in_proj",
    )(h, g, w)


def _da_kernel(slopes_ref, q_ref, k_ref, v_ref, lam_ref, gn_ref, o_ref, *, seq, tq, tk, lambda_init):
    h = pl.program_id(1)
    qi = pl.program_id(2)
    slope = slopes_ref[h]
    q = q_ref[...] * jnp.asarray(DA_HEAD_DIM ** -0.5, BF16)
    lane = lax.broadcasted_iota(jnp.int32, (tq, LANES), 1)
    zero = jnp.zeros_like(q)
    q1 = jnp.where(lane < DA_HEAD_DIM, q, zero)
    q2 = jnp.where(lane >= DA_HEAD_DIM, q, zero)
    rel = (lax.broadcasted_iota(jnp.int32, (tq, tk), 0)
           - lax.broadcasted_iota(jnp.int32, (tq, tk), 1)).astype(F32)
    q0 = qi * tq
    nt = (((1,), (1,)), ((), ()))

    def step(kc, carry):
        m1, l1, a1, m2, l2, a2 = carry
        k0 = pl.multiple_of(kc * tk, tk)
        k = k_ref[pl.ds(k0, tk), :]
        v = v_ref[pl.ds(k0, tk), :]
        bias = slope * jnp.abs(rel + (q0 - k0).astype(F32))

        def one(qm, m, l, a):
            s = lax.dot_general(qm, k, nt, preferred_element_type=F32) - bias
            m_new = jnp.maximum(m, jnp.max(s, axis=-1, keepdims=True))
            alpha = jnp.exp(m - m_new)
            p = jnp.exp(s - m_new)
            l_new = alpha * l + jnp.sum(p, axis=-1, keepdims=True)
            a_new = alpha * a + jnp.dot(p.astype(BF16), v, preferred_element_type=F32)
            return m_new, l_new, a_new

        m1, l1, a1 = one(q1, m1, l1, a1)
        m2, l2, a2 = one(q2, m2, l2, a2)
        return m1, l1, a1, m2, l2, a2

    minit = jnp.full((tq, 1), -jnp.inf, F32)
    linit = jnp.zeros((tq, 1), F32)
    ainit = jnp.zeros((tq, LANES), F32)
    m1, l1, a1, m2, l2, a2 = lax.fori_loop(
        0, seq // tk, step, (minit, linit, ainit, minit, linit, ainit))

    lm = lam_ref[...]
    lam = (jnp.exp(jnp.sum(lm[0:1] * lm[1:2], axis=-1, keepdims=True))
           - jnp.exp(jnp.sum(lm[2:3] * lm[3:4], axis=-1, keepdims=True)) + lambda_init)
    o = a1 / l1 - lam * (a2 / l2)
    o_ref[...] = (_rms(o, gn_ref[...]) * (1.0 - lambda_init)).astype(o_ref.dtype)


def _diff_attention(z, slopes, lam_p, gn, batch, seq, lambda_init, tq=256, tk=512):
    n = z.shape[0]
    nq = seq // tq
    kern = functools.partial(_da_kernel, seq=seq, tq=tq, tk=tk, lambda_init=lambda_init)
    return pl.pallas_call(
        kern,
        grid_spec=pltpu.PrefetchScalarGridSpec(
            num_scalar_prefetch=1,
            grid=(batch, DA_HEADS, nq),
            in_specs=[
                pl.BlockSpec((tq, LANES), lambda b, h, i, s: (b * nq + i, COL_QA + h)),
                pl.BlockSpec((seq, LANES), lambda b, h, i, s: (b, COL_KA + h)),
                pl.BlockSpec((seq, LANES), lambda b, h, i, s: (b, COL_VA + h)),
                pl.BlockSpec((4, DA_HEAD_DIM), lambda b, h, i, s: (0, 0)),
                pl.BlockSpec((1, LANES), lambda b, h, i, s: (0, 0)),
            ],
            out_specs=pl.BlockSpec((tq, LANES), lambda b, h, i, s: (b * nq + i, h)),
        ),
        out_shape=jax.ShapeDtypeStruct((n, DA_HEADS * LANES), BF16),
        compiler_params=_params("parallel", "parallel", "arbitrary"),
        name="diff_attention",
    )(slopes, z, z, z, lam_p, gn)


def _na_kernel(q_ref, k_ref, v_ref, t_ref, o_ref, *, rows, rb):
    r_base = pl.program_id(2) * rb
    lane = lax.broadcasted_iota(jnp.int32, (GRID_W, LANES), 1)
    first = lane < NA_HEAD_DIM
    scale = jnp.asarray(NA_HEAD_DIM ** -0.5, BF16)
    nt = (((1,), (1,)), ((), ()))
    win = NA_ROWS_MAX * GRID_W

    for j in range(rb):
        r = r_base + j
        ws = jnp.clip(r - NA_ROWS_MAX // 2, 0, rows - NA_ROWS_MAX)
        off = r - ws
        start = pl.multiple_of(ws * GRID_W, GRID_W)
        kw = k_ref[pl.ds(start, win), :]
        vw = v_ref[pl.ds(start, win), :]
        q = q_ref[j * GRID_W:(j + 1) * GRID_W, :] * scale
        zero = jnp.zeros_like(q)
        q2 = jnp.concatenate([jnp.where(first, q, zero), jnp.where(first, zero, q)], axis=0)
        s = lax.dot_general(q2, kw, nt, preferred_element_type=F32)
        s = jnp.concatenate(
            [s[:, jj * LANES:(jj + 1) * LANES] + t_ref[0, 2 * jj - off + (NA_ROWS_MAX - 1)]
             for jj in range(win // LANES)], axis=1)
        m = jnp.max(s, axis=-1, keepdims=True)
        p = jnp.exp(s - m)
        l = jnp.sum(p, axis=-1, keepdims=True)
        o = jnp.dot(p.astype(BF16), vw, preferred_element_type=F32) / l
        o_ref[j * GRID_W:(j + 1) * GRID_W, :] = jnp.where(
            first, o[:GRID_W], o[GRID_W:]).astype(o_ref.dtype)


def _na_bias_tables(rpb):
    c = jnp.arange(GRID_W)
    col_start = jnp.clip(c - NA_COLS // 2, 0, GRID_W - NA_COLS)
    in_win = (c[None, :] >= col_start[:, None]) & (c[None, :] < col_start[:, None] + NA_COLS)
    dc = jnp.clip(c[None, :] - c[:, None], -(NA_COLS - 1), NA_COLS - 1) + (NA_COLS - 1)
    tm = jnp.where(in_win[None, None], rpb[:, :, dc], NEG_BIG)
    nd = 2 * NA_ROWS_MAX - 2
    pair = jnp.stack([tm[:, :nd], tm[:, 1:nd + 1]], axis=3)
    pair = pair.reshape(NA_HEADS // 2, 2, nd, GRID_W, 2 * GRID_W)
    return pair.transpose(0, 2, 1, 3, 4).reshape(NA_HEADS // 2, nd, 2 * GRID_W, 2 * GRID_W)


def _neigh_attention(z, tables, batch, seq, rb=8):
    n = z.shape[0]
    rows = seq // GRID_W
    nblk = rows // rb
    tq = rb * GRID_W
    nd = tables.shape[1]
    kern = functools.partial(_na_kernel, rows=rows, rb=rb)
    return pl.pallas_call(
        kern,
        grid=(batch, NA_HEADS // 2, nblk),
        in_specs=[
            pl.BlockSpec((tq, LANES), lambda b, h, i: (b * nblk + i, COL_QN + h)),
            pl.BlockSpec((seq, LANES), lambda b, h, i: (b, COL_KN + h)),
            pl.BlockSpec((seq, LANES), lambda b, h, i: (b, COL_VN + h)),
            pl.BlockSpec((1, nd, LANES, LANES), lambda b, h, i: (h, 0, 0, 0)),
        ],
        out_specs=pl.BlockSpec((tq, LANES), lambda b, h, i: (b * nblk + i, h)),
        out_shape=jax.ShapeDtypeStruct((n, NA_HEADS * NA_HEAD_DIM), BF16),
        compiler_params=_params("parallel", "parallel", "arbitrary"),
        name="neigh_attention",
    )(z, z, z, tables)


def _rope_tables(seq):
    t = jnp.arange(seq)
    row = (t // GRID_W).astype(F32)
    col = (t % GRID_W).astype(F32)
    half = GQ_HEAD_DIM // 2
    freqs = ROPE_THETA ** (-jnp.arange(0, half, 2, dtype=F32) / half)
    ang = jnp.concatenate([row[:, None] * freqs, col[:, None] * freqs], axis=-1)
    cos = jnp.repeat(jnp.cos(ang), 2, axis=-1)
    sin = jnp.repeat(jnp.sin(ang), 2, axis=-1)
    sign = jnp.where(jnp.arange(GQ_HEAD_DIM) % 2 == 0, -1.0, 1.0).astype(F32)
    return cos, sin * sign


def _qk_rope_kernel(q_ref, k_ref, cos_ref, sin_ref, gq_ref, gk_ref, qo_ref, ko_ref):
    cos = cos_ref[...]
    sin = sin_ref[...]
    even = lax.broadcasted_iota(jnp.int32, cos.shape, 1) % 2 == 0

    def rope(x, g):
        y = _rms(x.astype(F32), g)
        swapped = jnp.where(even, pltpu.roll(y, LANES - 1, 1), pltpu.roll(y, 1, 1))
        return y * cos + swapped * sin

    scale = GQ_HEAD_DIM ** -0.5
    for hd in range(GQ_HEADS):
        sl = slice(hd * LANES, (hd + 1) * LANES)
        qo_ref[:, sl] = (rope(q_ref[:, sl], gq_ref[...]) * scale).astype(qo_ref.dtype)
    for hd in range(GQ_KV_HEADS):
        sl = slice(hd * LANES, (hd + 1) * LANES)
        ko_ref[:, sl] = rope(k_ref[:, sl], gk_ref[...]).astype(ko_ref.dtype)


def _qk_rope(z, cos, sin, gq, gk, seq, tm=512):
    n = z.shape[0]
    nper = seq // tm
    qw = GQ_HEADS * LANES
    kw = GQ_KV_HEADS * LANES
    return pl.pallas_call(
        _qk_rope_kernel,
        grid=(n // tm,),
        in_specs=[
            pl.BlockSpec((tm, qw), lambda i: (i, COL_QC * LANES // qw)),
            pl.BlockSpec((tm, kw), lambda i: (i, COL_KC * LANES // kw)),
            pl.BlockSpec((tm, LANES), lambda i: (i % nper, 0)),
            pl.BlockSpec((tm, LANES), lambda i: (i % nper, 0)),
            pl.BlockSpec((1, LANES), lambda i: (0, 0)),
            pl.BlockSpec((1, LANES), lambda i: (0, 0)),
        ],
        out_specs=[
            pl.BlockSpec((tm, qw), lambda i: (i, 0)),
            pl.BlockSpec((tm, kw), lambda i: (i, 0)),
        ],
        out_shape=[jax.ShapeDtypeStruct((n, qw), BF16), jax.ShapeDtypeStruct((n, kw), BF16)],
        compiler_params=_params("parallel"),
        name="qk_norm_rope",
    )(z, z, cos, sin, gq, gk)


def _gqa_kernel(q_ref, k_ref, v_ref, o_ref, *, seq, tq, tk):
    group = GQ_HEADS // GQ_KV_HEADS
    q = jnp.concatenate([q_ref[:, g * LANES:(g + 1) * LANES] for g in range(group)], axis=0)
    nt = (((1,), (1,)), ((), ()))
    mq = group * tq

    def step(kc, carry):
        m, l, a = carry
        k0 = pl.multiple_of(kc * tk, tk)
        k = k_ref[pl.ds(k0, tk), :]
        v = v_ref[pl.ds(k0, tk), :]
        s = lax.dot_general(q, k, nt, preferred_element_type=F32)
        m_new = jnp.maximum(m, jnp.max(s, axis=-1, keepdims=True))
        alpha = jnp.exp(m - m_new)
        p = jnp.exp(s - m_new)
        l_new = alpha * l + jnp.sum(p, axis=-1, keepdims=True)
        a_new = alpha * a + jnp.dot(p.astype(BF16), v, preferred_element_type=F32)
        return m_new, l_new, a_new

    m, l, a = lax.fori_loop(0, seq // tk, step, (
        jnp.full((mq, 1), -jnp.inf, F32), jnp.zeros((mq, 1), F32), jnp.zeros((mq, LANES), F32)))
    o = a / l
    for g in range(group):
        o_ref[:, g * LANES:(g + 1) * LANES] = o[g * tq:(g + 1) * tq].astype(o_ref.dtype)


def _gqa_attention(qr, kr, z, batch, seq, tq=256, tk=512):
    n = z.shape[0]
    nq = seq // tq
    gw = (GQ_HEADS // GQ_KV_HEADS) * LANES
    kern = functools.partial(_gqa_kernel, seq=seq, tq=tq, tk=tk)
    return pl.pallas_call(
        kern,
        grid=(batch, GQ_KV_HEADS, nq),
        in_specs=[
            pl.BlockSpec((tq, gw), lambda b, h, i: (b * nq + i, h)),
            pl.BlockSpec((seq, LANES), lambda b, h, i: (b, h)),
            pl.BlockSpec((seq, LANES), lambda b, h, i: (b, COL_VC + h)),
        ],
        out_specs=pl.BlockSpec((tq, gw), lambda b, h, i: (b * nq + i, h)),
        out_shape=jax.ShapeDtypeStruct((n, GQ_HEADS * LANES), BF16),
        compiler_params=_params("parallel", "parallel", "arbitrary"),
        name="gqa_attention",
    )(qr, kr, z)


def _merge_kernel(h_ref, oa_ref, ob_ref, oc_ref, g0_ref, g1_ref,
                  wa_ref, wb_ref, wc_ref, wo_ref, gp_ref, o_ref):
    d = D_MODEL
    g0 = g0_ref[...]
    g1 = g1_ref[...]
    gate_a = jax.nn.sigmoid(g0[:, :d].astype(F32))
    gate_b = jax.nn.sigmoid(jnp.concatenate([g0[:, d:], g1[:, :2 * d - g0.shape[1]]], axis=1).astype(F32))
    gate_c = jax.nn.sigmoid(g1[:, 2 * d - g0.shape[1]:].astype(F32))
    merged = gate_a * jnp.dot(oa_ref[...], wa_ref[...], preferred_element_type=F32)
    merged += gate_b * jnp.dot(ob_ref[...], wb_ref[...], preferred_element_type=F32)
    merged += gate_c * jnp.dot(oc_ref[...], wc_ref[...], preferred_element_type=F32)
    y = jnp.dot(merged.astype(BF16), wo_ref[...], preferred_element_type=F32)
    o_ref[...] = h_ref[...] + _rms(y, gp_ref[...])


def _merge(h, oa, ob, oc, z, wa, wb, wc, wo, gp, tm=512):
    n, d = h.shape
    gw = 3 * d // 2
    gblk = COL_GZ * LANES // gw
    row = lambda i: (i, 0)
    const = lambda i: (0, 0)
    return pl.pallas_call(
        _merge_kernel,
        grid=(n // tm,),
        in_specs=[
            pl.BlockSpec((tm, d), row),
            pl.BlockSpec((tm, d), row),
            pl.BlockSpec((tm, d), row),
            pl.BlockSpec((tm, d), row),
            pl.BlockSpec((tm, gw), lambda i: (i, gblk)),
            pl.BlockSpec((tm, gw), lambda i: (i, gblk + 1)),
            pl.BlockSpec((d, d), const),
            pl.BlockSpec((d, d), const),
            pl.BlockSpec((d, d), const),
            pl.BlockSpec((d, d), const),
            pl.BlockSpec((1, d), const),
        ],
        out_specs=pl.BlockSpec((tm, d), row),
        out_shape=jax.ShapeDtypeStruct((n, d), F32),
        compiler_params=_params("parallel"),
        name="gated_merge",
    )(h, oa, ob, oc, z, z, wa, wb, wc, wo, gp)


def _mlp_kernel(h_ref, g1_ref, wu_ref, wd_ref, g2_ref, o_ref, u_ref, acc_ref):
    j = pl.program_id(1)

    @pl.when(j == 0)
    def _():
        u_ref[...] = _rms(h_ref[...], g1_ref[...]).astype(BF16)
        acc_ref[...] = jnp.zeros_like(acc_ref)

    hid = jnp.dot(u_ref[...], wu_ref[...], preferred_element_type=F32)
    hid = jnp.square(jnp.maximum(hid, 0.0)).astype(BF16)
    acc_ref[...] += jnp.dot(hid, wd_ref[...], preferred_element_type=F32)

    @pl.when(j == pl.num_programs(1) - 1)
    def _():
        o_ref[...] = h_ref[...] + _rms(acc_ref[...], g2_ref[...])


def _mlp(h, g1, wu, wd, g2, tm=1024, tf=1024):
    n, d = h.shape
    dff = wu.shape[1]
    return pl.pallas_call(
        _mlp_kernel,
        grid=(n // tm, dff // tf),
        in_specs=[
            pl.BlockSpec((tm, d), lambda i, j: (i, 0)),
            pl.BlockSpec((1, d), lambda i, j: (0, 0)),
            pl.BlockSpec((d, tf), lambda i, j: (0, j)),
            pl.BlockSpec((tf, d), lambda i, j: (j, 0)),
            pl.BlockSpec((1, d), lambda i, j: (0, 0)),
        ],
        out_specs=pl.BlockSpec((tm, d), lambda i, j: (i, 0)),
        out_shape=jax.ShapeDtypeStruct((n, d), F32),
        scratch_shapes=[pltpu.VMEM((tm, d), BF16), pltpu.VMEM((tm, d), F32)],
        compiler_params=_params("parallel", "arbitrary"),
        name="relu2_mlp",
    )(h, g1, wu, wd, g2)


def _ple_kernel(h_ref, p_ref, wp_ref, wg_ref, g_ref, o_ref):
    h = h_ref[...]
    e = jnp.dot(p_ref[...].astype(BF16), wp_ref[...], preferred_element_type=F32)
    gate = jax.nn.sigmoid(jnp.dot(h.astype(BF16), wg_ref[...], preferred_element_type=F32))
    o_ref[...] = h + _rms(e * gate, g_ref[...])


def _ple(h, p, wp, wg, g, tm=1024):
    n, d = h.shape
    pd = p.shape[1]
    return pl.pallas_call(
        _ple_kernel,
        grid=(n // tm,),
        in_specs=[
            pl.BlockSpec((tm, d), lambda i: (i, 0)),
            pl.BlockSpec((tm, pd), lambda i: (i, 0)),
            pl.BlockSpec((pd, d), lambda i: (0, 0)),
            pl.BlockSpec((d, d), lambda i: (0, 0)),
            pl.BlockSpec((1, d), lambda i: (0, 0)),
        ],
        out_specs=pl.BlockSpec((tm, d), lambda i: (i, 0)),
        out_shape=jax.ShapeDtypeStruct((n, d), F32),
        compiler_params=_params("parallel"),
        name="gated_ple",
    )(h, p, wp, wg, g)


def _trunk(x, ple, w):
    batch, seq, d = x.shape
    n = batch * seq
    h = x.reshape(n, d)
    cos, sin = _rope_tables(seq)
    slopes = jnp.exp2(-8.0 * jnp.arange(1, DA_HEADS + 1, dtype=F32) / DA_HEADS)
    for i in range(DEPTH):
        lambda_init = 0.8 - 0.6 * math.exp(-0.3 * i)
        z = _norm_matmul(h, w["g_pre_mix"][i][None], w["w_in"][i])
        oa = _diff_attention(z, slopes, w["da_lambda"][i], w["da_norm"][i][None],
                             batch, seq, lambda_init)
        ob = _neigh_attention(z, w["na_tables"][i], batch, seq)
        qr, kr = _qk_rope(z, cos, sin, w["gq_q_norm"][i][None], w["gq_k_norm"][i][None], seq)
        oc = _gqa_attention(qr, kr, z, batch, seq)
        h = _merge(h, oa, ob, oc, z, w["w_br_a"][i], w["w_br_b"][i], w["w_br_c"][i],
                   w["w_o"][i], w["g_post_mix"][i][None])
        h = _mlp(h, w["g_pre_mlp"][i][None], w["w_up"][i], w["w_down"][i],
                 w["g_post_mlp"][i][None])
        h = _ple(h, ple[i].reshape(n, PLE_DIM), w["w_ple"][i], w["w_ple_gate"][i],
                 w["g_ple"][i][None])
    return h.reshape(batch, seq, d)


def kernel(x_prompt, x_sample, p_prompt, p_sample, w_in, da_lambda, da_norm, na_rpb, gq_q_norm, gq_k_norm, w_br_a, w_br_b, w_br_c, w_o, g_pre_mix, g_post_mix, g_pre_mlp, g_post_mlp, w_up, w_down, w_ple, w_ple_gate, g_ple):
    w = dict(
        w_in=w_in.astype(BF16), da_lambda=da_lambda, da_norm=da_norm,
        na_tables=jax.vmap(_na_bias_tables)(na_rpb),
        gq_q_norm=gq_q_norm, gq_k_norm=gq_k_norm,
        w_br_a=w_br_a.astype(BF16), w_br_b=w_br_b.astype(BF16), w_br_c=w_br_c.astype(BF16),
        w_o=w_o.astype(BF16), g_pre_mix=g_pre_mix, g_post_mix=g_post_mix,
        g_pre_mlp=g_pre_mlp, g_post_mlp=g_post_mlp,
        w_up=w_up.astype(BF16), w_down=w_down.astype(BF16),
        w_ple=w_ple.astype(BF16), w_ple_gate=w_ple_gate.astype(BF16), g_ple=g_ple,
    )
    return _trunk(x_prompt, p_prompt, w), _trunk(x_sample, p_sample, w)
```

```python
import functools
import math

import jax
import jax.numpy as jnp
from jax import lax
from jax.experimental import pallas as pl
from jax.experimental.pallas import tpu as pltpu

F32 = jnp.float32
BF16 = jnp.bfloat16

D_MODEL = 1024
DEPTH = 4
GRID_W = 64
EPS = 1e-6
DA_HEADS = 8
DA_HEAD_DIM = 64
NA_HEADS = 16
NA_HEAD_DIM = 64
NA_ROWS_MAX = 8
NA_COLS = 16
GQ_HEADS = 8
GQ_KV_HEADS = 2
GQ_HEAD_DIM = 128
ROPE_THETA = 10000.0
D_FF = 4 * D_MODEL
PLE_DIM = 256
IN_W = 10752

LANES = 128
COL_QA, COL_KA, COL_VA = 0, 8, 16
COL_QN, COL_KN, COL_VN = 24, 32, 40
COL_QC, COL_KC, COL_VC = 48, 56, 58
COL_GZ = 60
NEG_BIG = -1e30
LOG2E = math.log2(math.e)

VMEM_LIMIT = 48 * 1024 * 1024


def _params(*sem):
    return pltpu.CompilerParams(dimension_semantics=sem, vmem_limit_bytes=VMEM_LIMIT)


def _rms(x, g):
    return x * lax.rsqrt(jnp.mean(x * x, axis=-1, keepdims=True) + EPS) * g


def _norm_matmul_kernel(x_ref, g_ref, w_ref, cs_ref, o_ref, u_ref):
    @pl.when(pl.program_id(1) == 0)
    def _():
        u_ref[...] = _rms(x_ref[...], g_ref[...]).astype(BF16)

    acc = jnp.dot(u_ref[...], w_ref[...], preferred_element_type=F32)
    o_ref[...] = (acc * cs_ref[...]).astype(o_ref.dtype)


def _in_proj_col_scale():
    cs = jnp.ones((IN_W,), F32)
    cs = cs.at[COL_QA * LANES:COL_KA * LANES].set(DA_HEAD_DIM ** -0.5 * LOG2E)
    cs = cs.at[COL_QN * LANES:COL_KN * LANES].set(NA_HEAD_DIM ** -0.5 * LOG2E)
    return cs[None]


def _norm_matmul(h, g, w, cs, tm=1024, tn=1536):
    n, d = h.shape
    width = w.shape[1]
    return pl.pallas_call(
        _norm_matmul_kernel,
        grid=(n // tm, width // tn),
        in_specs=[
            pl.BlockSpec((tm, d), lambda i, j: (i, 0)),
            pl.BlockSpec((1, d), lambda i, j: (0, 0)),
            pl.BlockSpec((d, tn), lambda i, j: (0, j)),
            pl.BlockSpec((1, tn), lambda i, j: (0, j)),
        ],
        out_specs=pl.BlockSpec((tm, tn), lambda i, j: (i, j)),
        out_shape=jax.ShapeDtypeStruct((n, width), BF16),
        scratch_shapes=[pltpu.VMEM((tm, d), BF16)],
        compiler_params=_params("parallel", "arbitrary"),
        name="norm_in_proj",
    )(h, g, w, cs)


_NT = (((1,), (1,)), ((), ()))
_TN = (((0,), (0,)), ((), ()))


def _flash_transposed(q_maps, k_ref, v_ref, s_refs, p_refs, a_refs, bias_fn, seq, tk):
    nmaps = len(q_maps)
    nq = q_maps[0].shape[0]
    nk = seq // tk

    def scores(kc):
        k = k_ref[pl.ds(pl.multiple_of(kc * tk, tk), tk), :]
        bias = None if bias_fn is None else bias_fn(kc)
        mx = []
        for j in range(nmaps):
            s = lax.dot_general(k, q_maps[j], _NT, preferred_element_type=F32)
            if bias is not None:
                s = s - bias
            s_refs[j][...] = s
            mx.append(jnp.max(s, axis=0, keepdims=True))
        return tuple(mx)

    def trip(i, carry, last):
        ms, ls, mxs = carry
        v_prev = v_ref[pl.ds(pl.multiple_of(jnp.maximum(i - 1, 0) * tk, tk), tk), :]
        pvs = [lax.dot_general(v_prev, p_refs[j][...], _TN, preferred_element_type=F32)
               for j in range(nmaps)]
        m_new, l_new, ps, alphas = [], [], [], []
        for j in range(nmaps):
            mn = jnp.maximum(ms[j], mxs[j])
            alpha = jnp.exp2(ms[j] - mn)
            p = jnp.exp2(s_refs[j][...] - mn)
            m_new.append(mn)
            l_new.append(alpha * ls[j] + jnp.sum(p, axis=0, keepdims=True))
            ps.append(p.astype(BF16))
            alphas.append(alpha)
        mx_next = mxs if last else scores(i + 1)
        for j in range(nmaps):
            a_refs[j][...] = alphas[j] * (a_refs[j][...] + pvs[j])
            p_refs[j][...] = ps[j]
        return tuple(m_new), tuple(l_new), mx_next

    for j in range(nmaps):
        p_refs[j][...] = jnp.zeros_like(p_refs[j])
        a_refs[j][...] = jnp.zeros_like(a_refs[j])
    minit = tuple(jnp.full((1, nq), -jnp.inf, F32) for _ in range(nmaps))
    linit = tuple(jnp.zeros((1, nq), F32) for _ in range(nmaps))
    carry = (minit, linit, scores(0))
    carry = lax.fori_loop(0, nk - 1, lambda i, c: trip(i, c, False), carry)
    _, ls, _ = trip(nk - 1, carry, True)
    v_last = v_ref[pl.ds((nk - 1) * tk, tk), :]
    for j in range(nmaps):
        a_refs[j][...] += lax.dot_general(v_last, p_refs[j][...], _TN, preferred_element_type=F32)
    return ls


def _da_kernel(slopes_ref, q_ref, k_ref, v_ref, lam_ref, gn_ref, o_ref,
               s1_ref, s2_ref, p1_ref, p2_ref, a1_ref, a2_ref, *, seq, tq, tk, lambda_init):
    h = pl.program_id(1)
    q0 = pl.program_id(2) * tq
    slope = slopes_ref[h] * LOG2E
    q = q_ref[...]
    lane = lax.broadcasted_iota(jnp.int32, (tq, LANES), 1)
    zero = jnp.zeros_like(q)
    q1 = jnp.where(lane < DA_HEAD_DIM, q, zero)
    q2 = jnp.where(lane >= DA_HEAD_DIM, q, zero)
    srel = slope * (lax.broadcasted_iota(jnp.int32, (tk, tq), 1)
                    - lax.broadcasted_iota(jnp.int32, (tk, tq), 0)).astype(F32)

    def alibi(kc):
        return jnp.abs(srel + slope * (q0 - kc * tk).astype(F32))

    l1, l2 = _flash_transposed([q1, q2], k_ref, v_ref, (s1_ref, s2_ref), (p1_ref, p2_ref),
                               (a1_ref, a2_ref), alibi, seq, tk)
    lm = lam_ref[...]
    lam = (jnp.exp(jnp.sum(lm[0:1] * lm[1:2], axis=-1, keepdims=True))
           - jnp.exp(jnp.sum(lm[2:3] * lm[3:4], axis=-1, keepdims=True)) + lambda_init)
    o = (a1_ref[...] / l1 - lam * (a2_ref[...] / l2)).T
    o_ref[...] = (_rms(o, gn_ref[...]) * (1.0 - lambda_init)).astype(o_ref.dtype)


def _diff_attention(z, slopes, lam_p, gn, batch, seq, lambda_init, tq=512, tk=512):
    n = z.shape[0]
    nq = seq // tq
    kern = functools.partial(_da_kernel, seq=seq, tq=tq, tk=tk, lambda_init=lambda_init)
    return pl.pallas_call(
        kern,
        grid_spec=pltpu.PrefetchScalarGridSpec(
            num_scalar_prefetch=1,
            grid=(batch, DA_HEADS, nq),
            in_specs=[
                pl.BlockSpec((tq, LANES), lambda b, h, i, s: (b * nq + i, COL_QA + h)),
                pl.BlockSpec((seq, LANES), lambda b, h, i, s: (b, COL_KA + h)),
                pl.BlockSpec((seq, LANES), lambda b, h, i, s: (b, COL_VA + h)),
                pl.BlockSpec((4, DA_HEAD_DIM), lambda b, h, i, s: (0, 0)),
                pl.BlockSpec((1, LANES), lambda b, h, i, s: (0, 0)),
            ],
            out_specs=pl.BlockSpec((tq, LANES), lambda b, h, i, s: (b * nq + i, h)),
            scratch_shapes=[pltpu.VMEM((tk, tq), F32), pltpu.VMEM((tk, tq), F32),
                            pltpu.VMEM((tk, tq), BF16), pltpu.VMEM((tk, tq), BF16),
                            pltpu.VMEM((LANES, tq), F32), pltpu.VMEM((LANES, tq), F32)],
        ),
        out_shape=jax.ShapeDtypeStruct((n, DA_HEADS * LANES), BF16),
        compiler_params=_params("parallel", "parallel", "arbitrary"),
        name="diff_attention",
    )(slopes, z, z, z, lam_p, gn)


def _na_kernel(q_ref, k_ref, v_ref, t_ref, o_ref, *, rows, rb):
    r_base = pl.program_id(2) * rb
    lane = lax.broadcasted_iota(jnp.int32, (GRID_W, LANES), 1)
    first = lane < NA_HEAD_DIM
    nt =(((1,), (1,)), ((), ()))
    win = NA_ROWS_MAX * GRID_W

    for j in range(rb):
        r = r_base + j
        ws = jnp.clip(r - NA_ROWS_MAX // 2, 0, rows - NA_ROWS_MAX)
        off = r - ws
        start = pl.multiple_of(ws * GRID_W, GRID_W)
        kw = k_ref[pl.ds(start, win), :]
        vw = v_ref[pl.ds(start, win), :]
        q = q_ref[j * GRID_W:(j + 1) * GRID_W, :]
        zero = jnp.zeros_like(q)
        q2 = jnp.concatenate([jnp.where(first, q, zero), jnp.where(first, zero, q)], axis=0)
        s = lax.dot_general(q2, kw, nt, preferred_element_type=F32)
        s = jnp.concatenate(
            [s[:, jj * LANES:(jj + 1) * LANES] + t_ref[0, 2 * jj - off + (NA_ROWS_MAX - 1)]
             for jj in range(win // LANES)], axis=1)
        m = jnp.max(s, axis=-1, keepdims=True)
        p = jnp.exp2(s - m)
        l = jnp.sum(p, axis=-1, keepdims=True)
        o = jnp.dot(p.astype(BF16), vw, preferred_element_type=F32) / l
        o_ref[j * GRID_W:(j + 1) * GRID_W, :] = jnp.where(
            first, o[:GRID_W], o[GRID_W:]).astype(o_ref.dtype)


def _na_bias_tables(rpb):
    c = jnp.arange(GRID_W)
    col_start = jnp.clip(c - NA_COLS // 2, 0, GRID_W - NA_COLS)
    in_win = (c[None, :] >= col_start[:, None]) & (c[None, :] < col_start[:, None] + NA_COLS)
    dc = jnp.clip(c[None, :] - c[:, None], -(NA_COLS - 1), NA_COLS - 1) + (NA_COLS - 1)
    tm = jnp.where(in_win[None, None], rpb[:, :, dc] * LOG2E, NEG_BIG)
    nd = 2 * NA_ROWS_MAX - 2
    pair = jnp.stack([tm[:, :nd], tm[:, 1:nd + 1]], axis=3)
    pair = pair.reshape(NA_HEADS // 2, 2, nd, GRID_W, 2 * GRID_W)
    return pair.transpose(0, 2, 1, 3, 4).reshape(NA_HEADS // 2, nd, 2 * GRID_W, 2 * GRID_W)


def _neigh_attention(z, tables, batch, seq, rb=8):
    n = z.shape[0]
    rows = seq // GRID_W
    nblk = rows // rb
    tq = rb * GRID_W
    nd = tables.shape[1]
    kern = functools.partial(_na_kernel, rows=rows, rb=rb)
    return pl.pallas_call(
        kern,
        grid=(batch, NA_HEADS // 2, nblk),
        in_specs=[
            pl.BlockSpec((tq, LANES), lambda b, h, i: (b * nblk + i, COL_QN + h)),
            pl.BlockSpec((seq, LANES), lambda b, h, i: (b, COL_KN + h)),
            pl.BlockSpec((seq, LANES), lambda b, h, i: (b, COL_VN + h)),
            pl.BlockSpec((1, nd, LANES, LANES), lambda b, h, i: (h, 0, 0, 0)),
        ],
        out_specs=pl.BlockSpec((tq, LANES), lambda b, h, i: (b * nblk + i, h)),
        out_shape=jax.ShapeDtypeStruct((n, NA_HEADS * NA_HEAD_DIM), BF16),
        compiler_params=_params("parallel", "parallel", "arbitrary"),
        name="neigh_attention",
    )(z, z, z, tables)


def _rope_tables(seq):
    t = jnp.arange(seq)
    row = (t // GRID_W).astype(F32)
    col = (t % GRID_W).astype(F32)
    half = GQ_HEAD_DIM // 2
    freqs = ROPE_THETA ** (-jnp.arange(0, half, 2, dtype=F32) / half)
    ang = jnp.concatenate([row[:, None] * freqs, col[:, None] * freqs], axis=-1)
    cos = jnp.repeat(jnp.cos(ang), 2, axis=-1)
    sin = jnp.repeat(jnp.sin(ang), 2, axis=-1)
    sign = jnp.where(jnp.arange(GQ_HEAD_DIM) % 2 == 0, -1.0, 1.0).astype(F32)
    return cos, sin * sign


def _qk_rope_kernel(q_ref, k_ref, cos_ref, sin_ref, gq_ref, gk_ref, qo_ref, ko_ref):
    cos = cos_ref[...]
    sin = sin_ref[...]
    even = lax.broadcasted_iota(jnp.int32, cos.shape, 1) % 2 == 0

    def rope(x, g):
        y = _rms(x.astype(F32), g)
        swapped = jnp.where(even, pltpu.roll(y, LANES - 1, 1), pltpu.roll(y, 1, 1))
        return y * cos + swapped * sin

    scale = GQ_HEAD_DIM ** -0.5 * LOG2E
    for hd in range(GQ_HEADS):
        sl = slice(hd * LANES, (hd + 1) * LANES)
        qo_ref[:, sl] = (rope(q_ref[:, sl], gq_ref[...]) * scale).astype(qo_ref.dtype)
    for hd in range(GQ_KV_HEADS):
        sl = slice(hd * LANES, (hd + 1) * LANES)
        ko_ref[:, sl] = rope(k_ref[:, sl], gk_ref[...]).astype(ko_ref.dtype)


def _qk_rope(z, cos, sin, gq, gk, seq, tm=512):
    n = z.shape[0]
    nper = seq // tm
    qw = GQ_HEADS * LANES
    kw = GQ_KV_HEADS * LANES
    return pl.pallas_call(
        _qk_rope_kernel,
        grid=(n // tm,),
        in_specs=[
            pl.BlockSpec((tm, qw), lambda i: (i, COL_QC * LANES // qw)),
            pl.BlockSpec((tm, kw), lambda i: (i, COL_KC * LANES // kw)),
            pl.BlockSpec((tm, LANES), lambda i: (i % nper, 0)),
            pl.BlockSpec((tm, LANES), lambda i: (i % nper, 0)),
            pl.BlockSpec((1, LANES), lambda i: (0, 0)),
            pl.BlockSpec((1, LANES), lambda i: (0, 0)),
        ],
        out_specs=[
            pl.BlockSpec((tm, qw), lambda i: (i, 0)),
            pl.BlockSpec((tm, kw), lambda i: (i, 0)),
        ],
        out_shape=[jax.ShapeDtypeStruct((n, qw), BF16), jax.ShapeDtypeStruct((n, kw), BF16)],
        compiler_params=_params("parallel"),
        name="qk_norm_rope",
    )(z, z, cos, sin, gq, gk)


def _gqa_kernel(q_ref, k_ref, v_ref, o_ref, s_ref, p_ref, a_ref, *, seq, tq, tk):
    group = GQ_HEADS // GQ_KV_HEADS
    q = jnp.concatenate([q_ref[:, g * LANES:(g + 1) * LANES] for g in range(group)], axis=0)
    (l,) = _flash_transposed([q], k_ref, v_ref, (s_ref,), (p_ref,), (a_ref,), None, seq, tk)
    o = (a_ref[...] / l).T
    for g in range(group):
        o_ref[:, g * LANES:(g + 1) * LANES] = o[g * tq:(g + 1) * tq].astype(o_ref.dtype)


def _gqa_attention(qr, kr, z, batch, seq, tq=128, tk=512):
    n = z.shape[0]
    nq = seq // tq
    group = GQ_HEADS // GQ_KV_HEADS
    gw = group * LANES
    kern = functools.partial(_gqa_kernel, seq=seq, tq=tq, tk=tk)
    return pl.pallas_call(
        kern,
        grid=(batch, GQ_KV_HEADS, nq),
        in_specs=[
            pl.BlockSpec((tq, gw), lambda b, h, i: (b * nq + i, h)),
            pl.BlockSpec((seq, LANES), lambda b, h, i: (b, h)),
            pl.BlockSpec((seq, LANES), lambda b, h, i: (b, COL_VC + h)),
        ],
        out_specs=pl.BlockSpec((tq, gw), lambda b, h, i: (b * nq + i, h)),
        scratch_shapes=[pltpu.VMEM((tk, group * tq), F32), pltpu.VMEM((tk, group * tq), BF16),
                        pltpu.VMEM((LANES, group * tq), F32)],
        out_shape=jax.ShapeDtypeStruct((n, GQ_HEADS * LANES), BF16),
        compiler_params=_params("parallel", "parallel", "arbitrary"),
        name="gqa_attention",
    )(qr, kr, z)


def _merge_kernel(h_ref, oa_ref, ob_ref, oc_ref, g0_ref, g1_ref,
                  wa_ref, wb_ref, wc_ref, wo_ref, gp_ref, o_ref):
    d = D_MODEL
    g0 = g0_ref[...]
    g1 = g1_ref[...]
    gate_a = jax.nn.sigmoid(g0[:, :d].astype(F32))
    gate_b = jax.nn.sigmoid(jnp.concatenate([g0[:, d:], g1[:, :2 * d - g0.shape[1]]], axis=1).astype(F32))
    gate_c = jax.nn.sigmoid(g1[:, 2 * d - g0.shape[1]:].astype(F32))
    merged = gate_a * jnp.dot(oa_ref[...], wa_ref[...], preferred_element_type=F32)
    merged += gate_b * jnp.dot(ob_ref[...], wb_ref[...], preferred_element_type=F32)
    merged += gate_c * jnp.dot(oc_ref[...], wc_ref[...], preferred_element_type=F32)
    y = jnp.dot(merged.astype(BF16), wo_ref[...], preferred_element_type=F32)
    o_ref[...] = h_ref[...] + _rms(y, gp_ref[...])


def _merge(h, oa, ob, oc, z, wa, wb, wc, wo, gp, tm=512):
    n, d = h.shape
    gw = 3 * d // 2
    gblk = COL_GZ * LANES // gw
    row = lambda i: (i, 0)
    const = lambda i: (0, 0)
    return pl.pallas_call(
        _merge_kernel,
        grid=(n // tm,),
        in_specs=[
            pl.BlockSpec((tm, d), row),
            pl.BlockSpec((tm, d), row),
            pl.BlockSpec((tm, d), row),
            pl.BlockSpec((tm, d), row),
            pl.BlockSpec((tm, gw), lambda i: (i, gblk)),
            pl.BlockSpec((tm, gw), lambda i: (i, gblk + 1)),
            pl.BlockSpec((d, d), const),
            pl.BlockSpec((d, d), const),
            pl.BlockSpec((d, d), const),
            pl.BlockSpec((d, d), const),
            pl.BlockSpec((1, d), const),
        ],
        out_specs=pl.BlockSpec((tm, d), row),
        out_shape=jax.ShapeDtypeStruct((n, d), F32),
        compiler_params=_params("parallel"),
        name="gated_merge",
    )(h, oa, ob, oc, z, z, wa, wb, wc, wo, gp)


def _mlp_kernel(h_ref, g1_ref, wu_ref, wd_ref, g2_ref, o_ref, u_ref, acc_ref):
    j = pl.program_id(1)

    @pl.when(j == 0)
    def _():
        u_ref[...] = _rms(h_ref[...], g1_ref[...]).astype(BF16)
        acc_ref[...] = jnp.zeros_like(acc_ref)

    hid = jnp.dot(u_ref[...], wu_ref[...], preferred_element_type=F32)
    hid = jnp.square(jnp.maximum(hid, 0.0)).astype(BF16)
    acc_ref[...] += jnp.dot(hid, wd_ref[...], preferred_element_type=F32)

    @pl.when(j == pl.num_programs(1) - 1)
    def _():
        o_ref[...] = h_ref[...] + _rms(acc_ref[...], g2_ref[...])


def _mlp(h, g1, wu, wd, g2, tm=1024, tf=1024):
    n, d = h.shape
    dff = wu.shape[1]
    return pl.pallas_call(
        _mlp_kernel,
        grid=(n // tm, dff // tf),
        in_specs=[
            pl.BlockSpec((tm, d), lambda i, j: (i, 0)),
            pl.BlockSpec((1, d), lambda i, j: (0, 0)),
            pl.BlockSpec((d, tf), lambda i, j: (0, j)),
            pl.BlockSpec((tf, d), lambda i, j: (j, 0)),
            pl.BlockSpec((1, d), lambda i, j: (0, 0)),
        ],
        out_specs=pl.BlockSpec((tm, d), lambda i, j: (i, 0)),
        out_shape=jax.ShapeDtypeStruct((n, d), F32),
        scratch_shapes=[pltpu.VMEM((tm, d), BF16), pltpu.VMEM((tm, d), F32)],
        compiler_params=_params("parallel", "arbitrary"),
        name="relu2_mlp",
    )(h, g1, wu, wd, g2)


def _ple_kernel(h_ref, p_ref, wp_ref, wg_ref, g_ref, o_ref):
    h = h_ref[...]
    e = jnp.dot(p_ref[...].astype(BF16), wp_ref[...], preferred_element_type=F32)
    gate = jax.nn.sigmoid(jnp.dot(h.astype(BF16), wg_ref[...], preferred_element_type=F32))
    o_ref[...] = h + _rms(e * gate, g_ref[...])


def _ple(h, p, wp, wg, g, tm=1024):
    n, d = h.shape
    pd = p.shape[1]
    return pl.pallas_call(
        _ple_kernel,
        grid=(n // tm,),
        in_specs=[
            pl.BlockSpec((tm, d), lambda i: (i, 0)),
            pl.BlockSpec((tm, pd), lambda i: (i, 0)),
            pl.BlockSpec((pd, d), lambda i: (0, 0)),
            pl.BlockSpec((d, d), lambda i: (0, 0)),
            pl.BlockSpec((1, d), lambda i: (0, 0)),
        ],
        out_specs=pl.BlockSpec((tm, d), lambda i: (i, 0)),
        out_shape=jax.ShapeDtypeStruct((n, d), F32),
        compiler_params=_params("parallel"),
        name="gated_ple",
    )(h, p, wp, wg, g)


def _trunk(x, ple, w):
    batch, seq, d = x.shape
    n = batch * seq
    h = x.reshape(n, d)
    cos, sin = _rope_tables(seq)
    slopes = jnp.exp2(-8.0 * jnp.arange(1, DA_HEADS + 1, dtype=F32) / DA_HEADS)
    col_scale = _in_proj_col_scale()
    for i in range(DEPTH):
        lambda_init = 0.8 - 0.6 * math.exp(-0.3 * i)
        z = _norm_matmul(h, w["g_pre_mix"][i][None], w["w_in"][i], col_scale)
        oa = _diff_attention(z, slopes, w["da_lambda"][i], w["da_norm"][i][None],
                             batch, seq, lambda_init)
        ob = _neigh_attention(z, w["na_tables"][i], batch, seq)
        qr, kr = _qk_rope(z, cos, sin, w["gq_q_norm"][i][None], w["gq_k_norm"][i][None], seq)
        oc = _gqa_attention(qr, kr, z, batch, seq)
        h = _merge(h, oa, ob, oc, z, w["w_br_a"][i], w["w_br_b"][i], w["w_br_c"][i],
                   w["w_o"][i], w["g_post_mix"][i][None])
        h = _mlp(h, w["g_pre_mlp"][i][None], w["w_up"][i], w["w_down"][i],
                 w["g_post_mlp"][i][None])
        h = _ple(h, ple[i].reshape(n, PLE_DIM), w["w_ple"][i], w["w_ple_gate"][i],
                 w["g_ple"][i][None])
    return h.reshape(batch, seq, d)


def kernel(x_prompt, x_sample, p_prompt, p_sample, w_in, da_lambda, da_norm, na_rpb, gq_q_norm, gq_k_norm, w_br_a, w_br_b, w_br_c, w_o, g_pre_mix, g_post_mix, g_pre_mlp, g_post_mlp, w_up, w_down, w_ple, w_ple_gate, g_ple):
    w = dict(
        w_in=w_in.astype(BF16), da_lambda=da_lambda, da_norm=da_norm,
        na_tables=jax.vmap(_na_bias_tables)(na_rpb),
        gq_q_norm=gq_q_norm, gq_k_norm=gq_k_norm,
        w_br_a=w_br_a.astype(BF16), w_br_b=w_br_b.astype(BF16), w_br_c=w_br_c.astype(BF16),
        w_o=w_o.astype(BF16), g_pre_mix=g_pre_mix, g_post_mix=g_post_mix,
        g_pre_mlp=g_pre_mlp, g_post_mlp=g_post_mlp,
        w_up=w_up.astype(BF16), w_down=w_down.astype(BF16),
        w_ple=w_ple.astype(BF16), w_ple_gate=w_ple_gate.astype(BF16), g_ple=g_ple,
    )
    return _trunk(x_prompt, p_prompt, w), _trunk(x_sample, p_sample, w)
```

```python
import functools
import math

import jax
import jax.numpy as jnp
from jax import lax
from jax.experimental import pallas as pl
from jax.experimental.pallas import tpu as pltpu

F32 = jnp.float32
BF16 = jnp.bfloat16

D_MODEL = 1024
DEPTH = 4
GRID_W = 64
EPS = 1e-6
DA_HEADS = 8
DA_HEAD_DIM = 64
NA_HEADS = 16
NA_HEAD_DIM = 64
NA_ROWS_MAX = 8
NA_COLS = 16
GQ_HEADS = 8
GQ_KV_HEADS = 2
GQ_HEAD_DIM = 128
ROPE_THETA = 10000.0
D_FF = 4 * D_MODEL
PLE_DIM = 256
IN_W = 10752

LANES = 128
COL_QA, COL_KA, COL_VA = 0, 8, 16
COL_QN, COL_KN, COL_VN = 24, 32, 40
COL_QC, COL_KC, COL_VC = 48, 56, 58
COL_GZ = 60
NEG_BIG = -1e30
LOG2E = math.log2(math.e)

VMEM_LIMIT = 48 * 1024 * 1024


def _params(*sem):
    return pltpu.CompilerParams(dimension_semantics=sem, vmem_limit_bytes=VMEM_LIMIT)


def _rms(x, g):
    return x * lax.rsqrt(jnp.mean(x * x, axis=-1, keepdims=True) + EPS) * g


def _norm_matmul_kernel(x_ref, g_ref, w_ref, cs_ref, o_ref, u_ref):
    @pl.when(pl.program_id(1) == 0)
    def _():
        u_ref[...] = _rms(x_ref[...], g_ref[...]).astype(BF16)

    acc = jnp.dot(u_ref[...], w_ref[...], preferred_element_type=F32)
    o_ref[...] = (acc * cs_ref[...]).astype(o_ref.dtype)


def _in_proj_col_scale():
    cs = jnp.ones((IN_W,), F32)
    cs = cs.at[COL_QA * LANES:COL_KA * LANES].set(DA_HEAD_DIM ** -0.5 * LOG2E)
    cs = cs.at[COL_QN * LANES:COL_KN * LANES].set(NA_HEAD_DIM ** -0.5 * LOG2E)
    return cs[None]


def _norm_matmul(h, g, w, cs, tm=1024, tn=1536):
    n, d = h.shape
    width = w.shape[1]
    return pl.pallas_call(
        _norm_matmul_kernel,
        grid=(n // tm, width // tn),
        in_specs=[
            pl.BlockSpec((tm, d), lambda i, j: (i, 0)),
            pl.BlockSpec((1, d), lambda i, j: (0, 0)),
            pl.BlockSpec((d, tn), lambda i, j: (0, j)),
            pl.BlockSpec((1, tn), lambda i, j: (0, j)),
        ],
        out_specs=pl.BlockSpec((tm, tn), lambda i, j: (i, j)),
        out_shape=jax.ShapeDtypeStruct((n, width), BF16),
        scratch_shapes=[pltpu.VMEM((tm, d), BF16)],
        compiler_params=_params("parallel", "arbitrary"),
        name="norm_in_proj",
    )(h, g, w, cs)


_NT = (((1,), (1,)), ((), ()))
_TN = (((0,), (0,)), ((), ()))


def _transpose_values(v_ref, vt_ref, seq, chunk=512):
    @pl.when(pl.program_id(2) == 0)
    def _():
        for c in range(seq // chunk):
            vt_ref[:, c * chunk:(c + 1) * chunk] = v_ref[c * chunk:(c + 1) * chunk, :].T


def _flash_transposed(q_maps, k_ref, vt_ref, s_refs, p_refs, a_refs, seq, tk,
                      bias_fn=None, shift_fn=None, double=True):
    nmaps = len(q_maps)
    nq = q_maps[0].shape[0]
    nk = seq // tk

    def start(c):
        return c * tk if isinstance(c, int) else pl.multiple_of(c * tk, tk)

    def shift(c):
        return 0.0 if shift_fn is None else shift_fn(c)

    def scores(kc, slot):
        k = k_ref[pl.ds(start(kc), tk), :]
        bias = None if bias_fn is None else bias_fn(kc)
        mx = []
        for j in range(nmaps):
            s = lax.dot_general(k, q_maps[j], _NT, preferred_element_type=F32)
            if bias is not None:
                s = s - bias
            s_refs[j][slot] = s
            mx.append(jnp.max(s, axis=0, keepdims=True) - shift(kc))
        return tuple(mx)

    def values(i, pslot):
        prev = max(i - 1, 0) if isinstance(i, int) else jnp.maximum(i - 1, 0)
        vt_prev = vt_ref[:, pl.ds(start(prev), tk)]
        return [jnp.dot(vt_prev, p_refs[j][pslot], preferred_element_type=F32) for j in range(nmaps)]

    def softmax(i, slot, carry, pvs):
        ms, ls, mxs = carry
        m_new, l_new = [], []
        for j in range(nmaps):
            mn = jnp.maximum(ms[j], mxs[j])
            alpha = jnp.exp2(ms[j] - mn)
            p = jnp.exp2(s_refs[j][slot] - (mn + shift(i)))
            m_new.append(mn)
            l_new.append(alpha * ls[j] + jnp.sum(p, axis=0, keepdims=True))
            a_refs[j][...] = alpha * (a_refs[j][...] + pvs[j])
            p_refs[j][slot] = p.astype(BF16)
        return tuple(m_new), tuple(l_new)

    def trip(i, carry, slot=0, last=False):
        if double:
            mx_next = carry[2] if last else scores(i + 1, 1 - slot)
            m_new, l_new = softmax(i, slot, carry, values(i, 1 - slot))
        else:
            m_new, l_new = softmax(i, 0, carry, values(i, 0))
            mx_next = carry[2] if last else scores(i + 1, 0)
        return m_new, l_new, mx_next

    for j in range(nmaps):
        p_refs[j][...] = jnp.zeros_like(p_refs[j])
        a_refs[j][...] = jnp.zeros_like(a_refs[j])
    minit = tuple(jnp.full((1, nq), -jnp.inf, F32) for _ in range(nmaps))
    linit = tuple(jnp.zeros((1, nq), F32) for _ in range(nmaps))
    carry = (minit, linit, scores(0, 0))
    if double:
        looped = 2 * ((nk - 1) // 2)
        carry = lax.fori_loop(
            0, looped // 2, lambda jj, c: trip(2 * jj + 1, trip(2 * jj, c, 0), 1), carry)
    else:
        looped = nk - 1
        carry = lax.fori_loop(0, looped, trip, carry)
    for i in range(looped, nk - 1):
        carry = trip(i, carry, i % 2)
    last_slot = (nk - 1) % 2 if double else 0
    _, ls, _ = trip(nk - 1, carry, last_slot, last=True)
    vt_last = vt_ref[:, (nk - 1) * tk:]
    for j in range(nmaps):
        a_refs[j][...] += jnp.dot(vt_last, p_refs[j][last_slot], preferred_element_type=F32)
    return ls


def _da_kernel(slopes_ref, q_ref, k_ref, v_ref, lam_ref, gn_ref, o_ref,
               vt_ref, bm_ref, s1_ref, s2_ref, p1_ref, p2_ref, a1_ref, a2_ref,
               *, seq, tq, tk, lambda_init, double):
    assert tq == tk
    h = pl.program_id(1)
    q0 = pl.program_id(2) * tq
    slope = slopes_ref[h] * LOG2E
    q = q_ref[...]
    lane = lax.broadcasted_iota(jnp.int32, (tq, LANES), 1)
    zero = jnp.zeros_like(q)
    q1 = jnp.where(lane < DA_HEAD_DIM, q, zero)
    q2 = jnp.where(lane >= DA_HEAD_DIM, q, zero)

    @pl.when(pl.program_id(2) == 0)
    def _():
        srel = slope * (lax.broadcasted_iota(jnp.int32, (tk, tq), 1)
                        - lax.broadcasted_iota(jnp.int32, (tk, tq), 0)).astype(F32)
        bm_ref[0] = srel
        bm_ref[1] = jnp.abs(srel)
        bm_ref[2] = -srel

    def alibi_tile(kc):
        k0 = kc * tk
        return bm_ref[jnp.where(k0 < q0, 0, jnp.where(k0 == q0, 1, 2))]

    def alibi_shift(kc):
        return slope * jnp.abs((q0 - kc * tk).astype(F32))

    _transpose_values(v_ref, vt_ref, seq)
    l1, l2 = _flash_transposed([q1, q2], k_ref, vt_ref, (s1_ref, s2_ref), (p1_ref, p2_ref),
                               (a1_ref, a2_ref), seq, tk, bias_fn=alibi_tile,
                               shift_fn=alibi_shift, double=double)
    lm = lam_ref[...]
    lam = (jnp.exp(jnp.sum(lm[0:1] * lm[1:2], axis=-1, keepdims=True))
           - jnp.exp(jnp.sum(lm[2:3] * lm[3:4], axis=-1, keepdims=True)) + lambda_init)
    o = (a1_ref[...] / l1 - lam * (a2_ref[...] / l2)).T
    o_ref[...] = (_rms(o, gn_ref[...]) * (1.0 - lambda_init)).astype(o_ref.dtype)


def _diff_attention(z, slopes, lam_p, gn, batch, seq, lambda_init, tq=512, tk=512, double=False):
    n = z.shape[0]
    nq = seq // tq
    kern = functools.partial(_da_kernel, seq=seq, tq=tq, tk=tk, lambda_init=lambda_init,
                             double=double)
    nslot = 2 if double else 1
    return pl.pallas_call(
        kern,
        grid_spec=pltpu.PrefetchScalarGridSpec(
            num_scalar_prefetch=1,
            grid=(batch, DA_HEADS, nq),
            in_specs=[
                pl.BlockSpec((tq, LANES), lambda b, h, i, s: (b * nq + i, COL_QA + h)),
                pl.BlockSpec((seq, LANES), lambda b, h, i, s: (b, COL_KA + h)),
                pl.BlockSpec((seq, LANES), lambda b, h, i, s: (b, COL_VA + h)),
                pl.BlockSpec((4, DA_HEAD_DIM), lambda b, h, i, s: (0, 0)),
                pl.BlockSpec((1, LANES), lambda b, h, i, s: (0, 0)),
            ],
            out_specs=pl.BlockSpec((tq, LANES), lambda b, h, i, s: (b * nq + i, h)),
            scratch_shapes=[pltpu.VMEM((LANES, seq), BF16), pltpu.VMEM((3, tk, tq), F32),
                            pltpu.VMEM((nslot, tk, tq), F32), pltpu.VMEM((nslot, tk, tq), F32),
                            pltpu.VMEM((nslot, tk, tq), BF16), pltpu.VMEM((nslot, tk, tq), BF16),
                            pltpu.VMEM((LANES, tq), F32), pltpu.VMEM((LANES, tq), F32)],
        ),
        out_shape=jax.ShapeDtypeStruct((n, DA_HEADS * LANES), BF16),
        compiler_params=_params("parallel", "parallel", "arbitrary"),
        name="diff_attention",
    )(slopes, z, z, z, lam_p, gn)


NA_SLAB = NA_ROWS_MAX + 1
NA_MASKED = 2 * NA_ROWS_MAX - 1


def _na_kernel(q_ref, k_ref, v_ref, t_ref, o_ref, s_ref, p_ref, l_ref, *, rows):
    npair = rows // 2
    slab = NA_SLAB * GRID_W
    pair_w = 2 * GRID_W
    first = lax.broadcasted_iota(jnp.int32, (GRID_W, LANES), 1) < NA_HEAD_DIM

    def clip(x, lo, hi):
        return min(max(x, lo), hi) if isinstance(x, int) else jnp.clip(x, lo, hi)

    def rows_at(ref, row, nrows):
        off = row * GRID_W
        if not isinstance(off, int):
            off = pl.multiple_of(off, GRID_W)
        return ref.at[pl.ds(off, nrows * GRID_W), :]

    def slab_start(pi):
        return clip(2 * pi - NA_ROWS_MAX // 2, 0, rows - NA_SLAB)

    def scores(pi, slot):
        ws = slab_start(pi)
        kw = rows_at(k_ref, ws, NA_SLAB)[...]
        q = rows_at(q_ref, 2 * pi, 2)[...]
        zero = jnp.zeros_like(q)
        first2 = jnp.concatenate([first, first], axis=0)
        qa = jnp.where(first2, q, zero)
        qb = jnp.where(first2, zero, q)
        q4 = jnp.concatenate([qa[:GRID_W], qb[:GRID_W], qa[GRID_W:], qb[GRID_W:]], axis=0)
        s = lax.dot_general(kw, q4, _NT, preferred_element_type=F32)
        mx = []
        for rq in range(2):
            qrow = 2 * pi + rq
            wq = clip(qrow - NA_ROWS_MAX // 2, 0, rows - NA_ROWS_MAX)
            m = None
            for i in range(NA_SLAB):
                krow = ws + i
                valid = (krow >= wq) & (krow < wq + NA_ROWS_MAX)
                dr = krow - qrow + (NA_ROWS_MAX - 1)
                if isinstance(valid, bool):
                    idx = dr if valid else NA_MASKED
                else:
                    idx = jnp.where(valid, dr, NA_MASKED)
                blk = (s[i * GRID_W:(i + 1) * GRID_W, rq * LANES:(rq + 1) * LANES]
                       + t_ref[0, idx])
                s_ref[slot, i * GRID_W:(i + 1) * GRID_W, rq * LANES:(rq + 1) * LANES] = blk
                bm = jnp.max(blk, axis=0, keepdims=True)
                m = bm if m is None else jnp.maximum(m, bm)
            mx.append(m)
        return jnp.concatenate(mx, axis=1)

    def values(pi, pslot):
        prev = max(pi - 1, 0) if isinstance(pi, int) else jnp.maximum(pi - 1, 0)
        vw = rows_at(v_ref, slab_start(prev), NA_SLAB)[...]
        ot = lax.dot_general(vw, p_ref[pslot], _TN, preferred_element_type=F32) / l_ref[pslot]
        o = ot.T
        out = jnp.concatenate(
            [jnp.where(first, o[:GRID_W], o[GRID_W:pair_w]),
             jnp.where(first, o[pair_w:pair_w + GRID_W], o[pair_w + GRID_W:])], axis=0)
        rows_at(o_ref, 2 * prev, 2)[...] = out.astype(o_ref.dtype)

    def trip(pi, carry, slot=0, last=False):
        mx = carry
        mx_next = mx if last else scores(pi + 1, 1 - slot)
        values(pi, 1 - slot)
        p = jnp.exp2(s_ref[slot] - mx)
        p_ref[slot] = p.astype(BF16)
        l_ref[slot] = jnp.sum(p, axis=0, keepdims=True)
        return mx_next

    p_ref[...] = jnp.zeros_like(p_ref)
    l_ref[...] = jnp.ones_like(l_ref)
    carry = scores(0, 0)
    looped = 2 * ((npair - 1) // 2)
    carry = lax.fori_loop(0, looped // 2,
                          lambda jj, c: trip(2 * jj + 1, trip(2 * jj, c, 0), 1), carry)
    for pi in range(looped, npair - 1):
        carry = trip(pi, carry, pi % 2)
    trip(npair - 1, carry, (npair - 1) % 2, last=True)
    values(npair, (npair - 1) % 2)


def _na_bias_tables(rpb):
    c = jnp.arange(GRID_W)
    col_start = jnp.clip(c - NA_COLS // 2, 0, GRID_W - NA_COLS)
    in_win = (c[None, :] >= col_start[:, None]) & (c[None, :] < col_start[:, None] + NA_COLS)
    dc = jnp.clip(c[None, :] - c[:, None], -(NA_COLS - 1), NA_COLS - 1) + (NA_COLS - 1)
    tm = jnp.where(in_win[None, None], rpb[:, :, dc] * LOG2E, NEG_BIG)
    nd = tm.shape[1]
    tt = tm.reshape(NA_HEADS // 2, 2, nd, GRID_W, GRID_W).transpose(0, 2, 4, 1, 3)
    tt = tt.reshape(NA_HEADS // 2, nd, GRID_W, 2 * GRID_W)
    masked = jnp.full((NA_HEADS // 2, NA_MASKED + 1 - nd, GRID_W, 2 * GRID_W), NEG_BIG, F32)
    return jnp.concatenate([tt, masked], axis=1)


def _neigh_attention(z, tables, batch, seq):
    n = z.shape[0]
    slab = NA_SLAB * GRID_W
    kern = functools.partial(_na_kernel, rows=seq // GRID_W)

    def column(col):
        return pl.BlockSpec((seq, LANES), lambda b, h: (b, col + h))

    return pl.pallas_call(
        kern,
        grid=(batch, NA_HEADS // 2),
        in_specs=[column(COL_QN), column(COL_KN), column(COL_VN),
                  pl.BlockSpec((1,) + tables.shape[1:], lambda b, h: (h, 0, 0, 0))],
        out_specs=column(0),
        out_shape=jax.ShapeDtypeStruct((n, NA_HEADS * NA_HEAD_DIM), BF16),
        scratch_shapes=[pltpu.VMEM((2, slab, 2 * LANES), F32), pltpu.VMEM((2, slab, 2 * LANES), BF16),
                        pltpu.VMEM((2, 1, 2 * LANES), F32)],
        compiler_params=_params("parallel", "parallel"),
        name="neigh_attention",
    )(z, z, z, tables)


def _rope_tables(seq):
    t = jnp.arange(seq)
    row = (t // GRID_W).astype(F32)
    col = (t % GRID_W).astype(F32)
    half = GQ_HEAD_DIM // 2
    freqs = ROPE_THETA ** (-jnp.arange(0, half, 2, dtype=F32) / half)
    ang = jnp.concatenate([row[:, None] * freqs, col[:, None] * freqs], axis=-1)
    cos = jnp.repeat(jnp.cos(ang), 2, axis=-1)
    sin = jnp.repeat(jnp.sin(ang), 2, axis=-1)
    sign = jnp.where(jnp.arange(GQ_HEAD_DIM) % 2 == 0, -1.0, 1.0).astype(F32)
    return cos, sin * sign


def _qk_rope_kernel(q_ref, k_ref, cos_ref, sin_ref, gq_ref, gk_ref, qo_ref, ko_ref):
    cos = cos_ref[...]
    sin = sin_ref[...]
    even = lax.broadcasted_iota(jnp.int32, cos.shape, 1) % 2 == 0

    def rope(x, g):
        y = _rms(x.astype(F32), g)
        swapped = jnp.where(even, pltpu.roll(y, LANES - 1, 1), pltpu.roll(y, 1, 1))
        return y * cos + swapped * sin

    scale = GQ_HEAD_DIM ** -0.5 * LOG2E
    for hd in range(GQ_HEADS):
        sl = slice(hd * LANES, (hd + 1) * LANES)
        qo_ref[:, sl] = (rope(q_ref[:, sl], gq_ref[...]) * scale).astype(qo_ref.dtype)
    for hd in range(GQ_KV_HEADS):
        sl = slice(hd * LANES, (hd + 1) * LANES)
        ko_ref[:, sl] = rope(k_ref[:, sl], gk_ref[...]).astype(ko_ref.dtype)


def _qk_rope(z, cos, sin, gq, gk, seq, tm=512):
    n = z.shape[0]
    nper = seq // tm
    qw = GQ_HEADS * LANES
    kw = GQ_KV_HEADS * LANES
    return pl.pallas_call(
        _qk_rope_kernel,
        grid=(n // tm,),
        in_specs=[
            pl.BlockSpec((tm, qw), lambda i: (i, COL_QC * LANES // qw)),
            pl.BlockSpec((tm, kw), lambda i: (i, COL_KC * LANES // kw)),
            pl.BlockSpec((tm, LANES), lambda i: (i % nper, 0)),
            pl.BlockSpec((tm, LANES), lambda i: (i % nper, 0)),
            pl.BlockSpec((1, LANES), lambda i: (0, 0)),
            pl.BlockSpec((1, LANES), lambda i: (0, 0)),
        ],
        out_specs=[
            pl.BlockSpec((tm, qw), lambda i: (i, 0)),
            pl.BlockSpec((tm, kw), lambda i: (i, 0)),
        ],
        out_shape=[jax.ShapeDtypeStruct((n, qw), BF16), jax.ShapeDtypeStruct((n, kw), BF16)],
        compiler_params=_params("parallel"),
        name="qk_norm_rope",
    )(z, z, cos, sin, gq, gk)


def _gqa_kernel(q_ref, k_ref, v_ref, o_ref, vt_ref, s_ref, p_ref, a_ref, *, seq, tq, tk):
    group = GQ_HEADS // GQ_KV_HEADS
    q = jnp.concatenate([q_ref[:, g * LANES:(g + 1) * LANES] for g in range(group)], axis=0)
    _transpose_values(v_ref, vt_ref, seq)
    (l,) = _flash_transposed([q], k_ref, vt_ref, (s_ref,), (p_ref,), (a_ref,), seq, tk)
    o = (a_ref[...] / l).T
    for g in range(group):
        o_ref[:, g * LANES:(g + 1) * LANES] = o[g * tq:(g + 1) * tq].astype(o_ref.dtype)


def _gqa_attention(qr, kr, z, batch, seq, tq=128, tk=1024):
    n = z.shape[0]
    nq = seq // tq
    group = GQ_HEADS // GQ_KV_HEADS
    gw = group * LANES
    kern = functools.partial(_gqa_kernel, seq=seq, tq=tq, tk=tk)
    return pl.pallas_call(
        kern,
        grid=(batch, GQ_KV_HEADS, nq),
        in_specs=[
            pl.BlockSpec((tq, gw), lambda b, h, i: (b * nq + i, h)),
            pl.BlockSpec((seq, LANES), lambda b, h, i: (b, h)),
            pl.BlockSpec((seq, LANES), lambda b, h, i: (b, COL_VC + h)),
        ],
        out_specs=pl.BlockSpec((tq, gw), lambda b, h, i: (b * nq + i, h)),
        scratch_shapes=[pltpu.VMEM((LANES, seq), BF16),
                        pltpu.VMEM((2, tk, group * tq), F32), pltpu.VMEM((2, tk, group * tq), BF16),
                        pltpu.VMEM((LANES, group * tq), F32)],
        out_shape=jax.ShapeDtypeStruct((n, GQ_HEADS * LANES), BF16),
        compiler_params=_params("parallel", "parallel", "arbitrary"),
        name="gqa_attention",
    )(qr, kr, z)


def _merge_kernel(h_ref, oa_ref, ob_ref, oc_ref, g0_ref, g1_ref,
                  wa_ref, wb_ref, wc_ref, wo_ref, gp_ref, o_ref):
    d = D_MODEL
    g0 = g0_ref[...]
    g1 = g1_ref[...]
    gate_a = jax.nn.sigmoid(g0[:, :d].astype(F32))
    gate_b = jax.nn.sigmoid(jnp.concatenate([g0[:, d:], g1[:, :2 * d - g0.shape[1]]], axis=1).astype(F32))
    gate_c = jax.nn.sigmoid(g1[:, 2 * d - g0.shape[1]:].astype(F32))
    merged = gate_a * jnp.dot(oa_ref[...], wa_ref[...], preferred_element_type=F32)
    merged += gate_b * jnp.dot(ob_ref[...], wb_ref[...], preferred_element_type=F32)
    merged += gate_c * jnp.dot(oc_ref[...], wc_ref[...], preferred_element_type=F32)
    y = jnp.dot(merged.astype(BF16), wo_ref[...], preferred_element_type=F32)
    o_ref[...] = h_ref[...] + _rms(y, gp_ref[...])


def _merge(h, oa, ob, oc, z, wa, wb, wc, wo, gp, tm=512):
    n, d = h.shape
    gw = 3 * d // 2
    gblk = COL_GZ * LANES // gw
    row = lambda i: (i, 0)
    const = lambda i: (0, 0)
    return pl.pallas_call(
        _merge_kernel,
        grid=(n // tm,),
        in_specs=[
            pl.BlockSpec((tm, d), row),
            pl.BlockSpec((tm, d), row),
            pl.BlockSpec((tm, d), row),
            pl.BlockSpec((tm, d), row),
            pl.BlockSpec((tm, gw), lambda i: (i, gblk)),
            pl.BlockSpec((tm, gw), lambda i: (i, gblk + 1)),
            pl.BlockSpec((d, d), const),
            pl.BlockSpec((d, d), const),
            pl.BlockSpec((d, d), const),
            pl.BlockSpec((d, d), const),
            pl.BlockSpec((1, d), const),
        ],
        out_specs=pl.BlockSpec((tm, d), row),
        out_shape=jax.ShapeDtypeStruct((n, d), F32),
        compiler_params=_params("parallel"),
        name="gated_merge",
    )(h, oa, ob, oc, z, z, wa, wb, wc, wo, gp)


def _mlp_kernel(h_ref, g1_ref, wu_ref, wd_ref, g2_ref, o_ref, u_ref, acc_ref):
    j = pl.program_id(1)

    @pl.when(j == 0)
    def _():
        u_ref[...] = _rms(h_ref[...], g1_ref[...]).astype(BF16)
        acc_ref[...] = jnp.zeros_like(acc_ref)

    hid = jnp.dot(u_ref[...], wu_ref[...], preferred_element_type=F32)
    hid = jnp.square(jnp.maximum(hid, 0.0)).astype(BF16)
    acc_ref[...] += jnp.dot(hid, wd_ref[...], preferred_element_type=F32)

    @pl.when(j == pl.num_programs(1) - 1)
    def _():
        o_ref[...] = h_ref[...] + _rms(acc_ref[...], g2_ref[...])


def _mlp(h, g1, wu, wd, g2, tm=1024, tf=1024):
    n, d = h.shape
    dff = wu.shape[1]
    return pl.pallas_call(
        _mlp_kernel,
        grid=(n // tm, dff // tf),
        in_specs=[
            pl.BlockSpec((tm, d), lambda i, j: (i, 0)),
            pl.BlockSpec((1, d), lambda i, j: (0, 0)),
            pl.BlockSpec((d, tf), lambda i, j: (0, j)),
            pl.BlockSpec((tf, d), lambda i, j: (j, 0)),
            pl.BlockSpec((1, d), lambda i, j: (0, 0)),
        ],
        out_specs=pl.BlockSpec((tm, d), lambda i, j: (i, 0)),
        out_shape=jax.ShapeDtypeStruct((n, d), F32),
        scratch_shapes=[pltpu.VMEM((tm, d), BF16), pltpu.VMEM((tm, d), F32)],
        compiler_params=_params("parallel", "arbitrary"),
        name="relu2_mlp",
    )(h, g1, wu, wd, g2)


def _ple_kernel(h_ref, p_ref, wp_ref, wg_ref, g_ref, o_ref):
    h = h_ref[...]
    e = jnp.dot(p_ref[...].astype(BF16), wp_ref[...], preferred_element_type=F32)
    gate = jax.nn.sigmoid(jnp.dot(h.astype(BF16), wg_ref[...], preferred_element_type=F32))
    o_ref[...] = h + _rms(e * gate, g_ref[...])


def _ple(h, p, wp, wg, g, tm=1024):
    n, d = h.shape
    pd = p.shape[1]
    return pl.pallas_call(
        _ple_kernel,
        grid=(n // tm,),
        in_specs=[
            pl.BlockSpec((tm, d), lambda i: (i, 0)),
            pl.BlockSpec((tm, pd), lambda i: (i, 0)),
            pl.BlockSpec((pd, d), lambda i: (0, 0)),
            pl.BlockSpec((d, d), lambda i: (0, 0)),
            pl.BlockSpec((1, d), lambda i: (0, 0)),
        ],
        out_specs=pl.BlockSpec((tm, d), lambda i: (i, 0)),
        out_shape=jax.ShapeDtypeStruct((n, d), F32),
        compiler_params=_params("parallel"),
        name="gated_ple",
    )(h, p, wp, wg, g)


def _trunk(x, ple, w):
    batch, seq, d = x.shape
    n = batch * seq
    h = x.reshape(n, d)
    cos, sin = _rope_tables(seq)
    slopes = jnp.exp2(-8.0 * jnp.arange(1, DA_HEADS + 1, dtype=F32) / DA_HEADS)
    col_scale = _in_proj_col_scale()
    for i in range(DEPTH):
        lambda_init = 0.8 - 0.6 * math.exp(-0.3 * i)
        z = _norm_matmul(h, w["g_pre_mix"][i][None], w["w_in"][i], col_scale)
        oa = _diff_attention(z, slopes, w["da_lambda"][i], w["da_norm"][i][None],
                             batch, seq, lambda_init)
        ob = _neigh_attention(z, w["na_tables"][i], batch, seq)
        qr, kr = _qk_rope(z, cos, sin, w["gq_q_norm"][i][None], w["gq_k_norm"][i][None], seq)
        oc = _gqa_attention(qr, kr, z, batch, seq)
        h = _merge(h, oa, ob, oc, z, w["w_br_a"][i], w["w_br_b"][i], w["w_br_c"][i],
                   w["w_o"][i], w["g_post_mix"][i][None])
        h = _mlp(h, w["g_pre_mlp"][i][None], w["w_up"][i], w["w_down"][i],
                 w["g_post_mlp"][i][None])
        h = _ple(h, ple[i].reshape(n, PLE_DIM), w["w_ple"][i], w["w_ple_gate"][i],
                 w["g_ple"][i][None])
    return h.reshape(batch, seq, d)


def kernel(x_prompt, x_sample, p_prompt, p_sample, w_in, da_lambda, da_norm, na_rpb, gq_q_norm, gq_k_norm, w_br_a, w_br_b, w_br_c, w_o, g_pre_mix, g_post_mix, g_pre_mlp, g_post_mlp, w_up, w_down, w_ple, w_ple_gate, g_ple):
    w = dict(
        w_in=w_in.astype(BF16), da_lambda=da_lambda, da_norm=da_norm,
        na_tables=jax.vmap(_na_bias_tables)(na_rpb),
        gq_q_norm=gq_q_norm, gq_k_norm=gq_k_norm,
        w_br_a=w_br_a.astype(BF16), w_br_b=w_br_b.astype(BF16), w_br_c=w_br_c.astype(BF16),
        w_o=w_o.astype(BF16), g_pre_mix=g_pre_mix, g_post_mix=g_post_mix,
        g_pre_mlp=g_pre_mlp, g_post_mlp=g_post_mlp,
        w_up=w_up.astype(BF16), w_down=w_down.astype(BF16),
        w_ple=w_ple.astype(BF16), w_ple_gate=w_ple_gate.astype(BF16), g_ple=g_ple,
    )
    return _trunk(x_prompt, p_prompt, w), _trunk(x_sample, p_sample, w)
```

```python
import functools
import math

import jax
import jax.numpy as jnp
from jax import lax
from jax.experimental import pallas as pl
from jax.experimental.pallas import tpu as pltpu

F32 = jnp.float32
BF16 = jnp.bfloat16

D_MODEL = 1024
DEPTH = 4
GRID_W = 64
EPS = 1e-6
DA_HEADS = 8
DA_HEAD_DIM = 64
NA_HEADS = 16
NA_HEAD_DIM = 64
NA_ROWS_MAX = 8
NA_COLS = 16
GQ_HEADS = 8
GQ_KV_HEADS = 2
GQ_HEAD_DIM = 128
ROPE_THETA = 10000.0
D_FF = 4 * D_MODEL
PLE_DIM = 256
IN_W = 10752

LANES = 128
COL_QA, COL_KA, COL_VA = 0, 8, 16
COL_QN, COL_KN, COL_VN = 24, 32, 40
COL_QC, COL_KC, COL_VC = 48, 56, 58
COL_GZ = 60
NEG_BIG = -1e30
LOG2E = math.log2(math.e)

VMEM_LIMIT = 48 * 1024 * 1024


def _params(*sem):
    return pltpu.CompilerParams(dimension_semantics=sem, vmem_limit_bytes=VMEM_LIMIT)


def _rms(x, g):
    return x * lax.rsqrt(jnp.mean(x * x, axis=-1, keepdims=True) + EPS) * g


def _norm_matmul_kernel(x_ref, g_ref, w_ref, cs_ref, o_ref, u_ref):
    @pl.when(pl.program_id(1) == 0)
    def _():
        u_ref[...] = _rms(x_ref[...], g_ref[...]).astype(BF16)

    acc = jnp.dot(u_ref[...], w_ref[...], preferred_element_type=F32)
    o_ref[...] = (acc * cs_ref[...]).astype(o_ref.dtype)


def _in_proj_col_scale():
    cs = jnp.ones((IN_W,), F32)
    cs = cs.at[COL_QA * LANES:COL_KA * LANES].set(DA_HEAD_DIM ** -0.5 * LOG2E)
    cs = cs.at[COL_QN * LANES:COL_KN * LANES].set(NA_HEAD_DIM ** -0.5 * LOG2E)
    return cs[None]


def _norm_matmul(h, g, w, cs, tm=1024, tn=1536):
    n, d = h.shape
    width = w.shape[1]
    return pl.pallas_call(
        _norm_matmul_kernel,
        grid=(n // tm, width // tn),
        in_specs=[
            pl.BlockSpec((tm, d), lambda i, j: (i, 0)),
            pl.BlockSpec((1, d), lambda i, j: (0, 0)),
            pl.BlockSpec((d, tn), lambda i, j: (0, j)),
            pl.BlockSpec((1, tn), lambda i, j: (0, j)),
        ],
        out_specs=pl.BlockSpec((tm, tn), lambda i, j: (i, j)),
        out_shape=jax.ShapeDtypeStruct((n, width), BF16),
        scratch_shapes=[pltpu.VMEM((tm, d), BF16)],
        compiler_params=_params("parallel", "arbitrary"),
        name="norm_in_proj",
    )(h, g, w, cs)


_NT = (((1,), (1,)), ((), ()))
_TN = (((0,), (0,)), ((), ()))


def _transpose_values(v_ref, vt_ref, seq, chunk=512):
    @pl.when(pl.program_id(2) == 0)
    def _():
        for c in range(seq // chunk):
            vt_ref[:, c * chunk:(c + 1) * chunk] = v_ref[c * chunk:(c + 1) * chunk, :].T


def _flash_transposed(q_maps, k_ref, vt_ref, s_refs, p_refs, a_refs, seq, tk,
                      bias_fn=None, shift_fn=None, double=True):
    nmaps = len(q_maps)
    nq = q_maps[0].shape[0]
    nk = seq // tk

    def start(c):
        return c * tk if isinstance(c, int) else pl.multiple_of(c * tk, tk)

    def shift(c):
        return 0.0 if shift_fn is None else shift_fn(c)

    def scores(kc, slot):
        k = k_ref[pl.ds(start(kc), tk), :]
        bias = None if bias_fn is None else bias_fn(kc)
        mx = []
        for j in range(nmaps):
            s = lax.dot_general(k, q_maps[j], _NT, preferred_element_type=F32)
            if bias is not None:
                s = s - bias
            s_refs[j][slot] = s
            mx.append(jnp.max(s, axis=0, keepdims=True) - shift(kc))
        return tuple(mx)

    def values(i, pslot):
        prev = max(i - 1, 0) if isinstance(i, int) else jnp.maximum(i - 1, 0)
        vt_prev = vt_ref[:, pl.ds(start(prev), tk)]
        return [jnp.dot(vt_prev, p_refs[j][pslot], preferred_element_type=F32) for j in range(nmaps)]

    def softmax(i, slot, carry, pvs):
        ms, ls, mxs = carry
        m_new, l_new = [], []
        for j in range(nmaps):
            mn = jnp.maximum(ms[j], mxs[j])
            alpha = jnp.exp2(ms[j] - mn)
            p = jnp.exp2(s_refs[j][slot] - (mn + shift(i)))
            m_new.append(mn)
            l_new.append(alpha * ls[j] + jnp.sum(p, axis=0, keepdims=True))
            a_refs[j][...] = alpha * (a_refs[j][...] + pvs[j])
            p_refs[j][slot] = p.astype(BF16)
        return tuple(m_new), tuple(l_new)

    def trip(i, carry, slot=0, last=False):
        if double:
            mx_next = carry[2] if last else scores(i + 1, 1 - slot)
            m_new, l_new = softmax(i, slot, carry, values(i, 1 - slot))
        else:
            m_new, l_new = softmax(i, 0, carry, values(i, 0))
            mx_next = carry[2] if last else scores(i + 1, 0)
        return m_new, l_new, mx_next

    for j in range(nmaps):
        p_refs[j][...] = jnp.zeros_like(p_refs[j])
        a_refs[j][...] = jnp.zeros_like(a_refs[j])
    minit = tuple(jnp.full((1, nq), -jnp.inf, F32) for _ in range(nmaps))
    linit = tuple(jnp.zeros((1, nq), F32) for _ in range(nmaps))
    carry = (minit, linit, scores(0, 0))
    if double:
        looped = 2 * ((nk - 1) // 2)
        carry = lax.fori_loop(
            0, looped // 2, lambda jj, c: trip(2 * jj + 1, trip(2 * jj, c, 0), 1), carry)
    else:
        looped = nk - 1
        carry = lax.fori_loop(0, looped, trip, carry)
    for i in range(looped, nk - 1):
        carry = trip(i, carry, i % 2)
    last_slot = (nk - 1) % 2 if double else 0
    _, ls, _ = trip(nk - 1, carry, last_slot, last=True)
    vt_last = vt_ref[:, (nk - 1) * tk:]
    for j in range(nmaps):
        a_refs[j][...] += jnp.dot(vt_last, p_refs[j][last_slot], preferred_element_type=F32)
    return ls


def _da_kernel(slopes_ref, q_ref, k_ref, v_ref, lam_ref, gn_ref, o_ref,
               vt_ref, bm_ref, s1_ref, s2_ref, p1_ref, p2_ref, a1_ref, a2_ref,
               *, seq, tq, tk, lambda_init, double):
    assert tq == tk
    h = pl.program_id(1)
    q0 = pl.program_id(2) * tq
    slope = slopes_ref[h] * LOG2E
    q = q_ref[...]
    lane = lax.broadcasted_iota(jnp.int32, (tq, LANES), 1)
    zero = jnp.zeros_like(q)
    q1 = jnp.where(lane < DA_HEAD_DIM, q, zero)
    q2 = jnp.where(lane >= DA_HEAD_DIM, q, zero)

    @pl.when(pl.program_id(2) == 0)
    def _():
        srel = slope * (lax.broadcasted_iota(jnp.int32, (tk, tq), 1)
                        - lax.broadcasted_iota(jnp.int32, (tk, tq), 0)).astype(F32)
        bm_ref[0] = srel
        bm_ref[1] = jnp.abs(srel)
        bm_ref[2] = -srel

    def alibi_tile(kc):
        k0 = kc * tk
        return bm_ref[jnp.where(k0 < q0, 0, jnp.where(k0 == q0, 1, 2))]

    def alibi_shift(kc):
        return slope * jnp.abs((q0 - kc * tk).astype(F32))

    _transpose_values(v_ref, vt_ref, seq)
    l1, l2 = _flash_transposed([q1, q2], k_ref, vt_ref, (s1_ref, s2_ref), (p1_ref, p2_ref),
                               (a1_ref, a2_ref), seq, tk, bias_fn=alibi_tile,
                               shift_fn=alibi_shift, double=double)
    lm = lam_ref[...]
    lam = (jnp.exp(jnp.sum(lm[0:1] * lm[1:2], axis=-1, keepdims=True))
           - jnp.exp(jnp.sum(lm[2:3] * lm[3:4], axis=-1, keepdims=True)) + lambda_init)
    o = (a1_ref[...] / l1 - lam * (a2_ref[...] / l2)).T
    o_ref[...] = (_rms(o, gn_ref[...]) * (1.0 - lambda_init)).astype(o_ref.dtype)


DA_MAX_TILE = 1024


def _diff_attention(z, slopes, lam_p, gn, batch, seq, lambda_init, tile=None, double=False):
    n = z.shape[0]
    if tile is None:
        tile = min(DA_MAX_TILE, seq // 4)
    tq = tk = tile
    nq = seq // tq
    kern = functools.partial(_da_kernel, seq=seq, tq=tq, tk=tk, lambda_init=lambda_init,
                             double=double)
    nslot = 2 if double else 1
    return pl.pallas_call(
        kern,
        grid_spec=pltpu.PrefetchScalarGridSpec(
            num_scalar_prefetch=1,
            grid=(batch, DA_HEADS, nq),
            in_specs=[
                pl.BlockSpec((tq, LANES), lambda b, h, i, s: (b * nq + i, COL_QA + h)),
                pl.BlockSpec((seq, LANES), lambda b, h, i, s: (b, COL_KA + h)),
                pl.BlockSpec((seq, LANES), lambda b, h, i, s: (b, COL_VA + h)),
                pl.BlockSpec((4, DA_HEAD_DIM), lambda b, h, i, s: (0, 0)),
                pl.BlockSpec((1, LANES), lambda b, h, i, s: (0, 0)),
            ],
            out_specs=pl.BlockSpec((tq, LANES), lambda b, h, i, s: (b * nq + i, h)),
            scratch_shapes=[pltpu.VMEM((LANES, seq), BF16), pltpu.VMEM((3, tk, tq), F32),
                            pltpu.VMEM((nslot, tk, tq), F32), pltpu.VMEM((nslot, tk, tq), F32),
                            pltpu.VMEM((nslot, tk, tq), BF16), pltpu.VMEM((nslot, tk, tq), BF16),
                            pltpu.VMEM((LANES, tq), F32), pltpu.VMEM((LANES, tq), F32)],
        ),
        out_shape=jax.ShapeDtypeStruct((n, DA_HEADS * LANES), BF16),
        compiler_params=_params("parallel", "parallel", "arbitrary"),
        name="diff_attention",
    )(slopes, z, z, z, lam_p, gn)


NA_SLAB = NA_ROWS_MAX + 1
NA_MASKED = 2 * NA_ROWS_MAX - 1


def _na_kernel(q_ref, k_ref, v_ref, t_ref, o_ref, s_ref, p_ref, l_ref, *, rows):
    npair = rows // 2
    slab = NA_SLAB * GRID_W
    pair_w = 2 * GRID_W
    first = lax.broadcasted_iota(jnp.int32, (GRID_W, LANES), 1) < NA_HEAD_DIM

    def clip(x, lo, hi):
        return min(max(x, lo), hi) if isinstance(x, int) else jnp.clip(x, lo, hi)

    def rows_at(ref, row, nrows):
        off = row * GRID_W
        if not isinstance(off, int):
            off = pl.multiple_of(off, GRID_W)
        return ref.at[pl.ds(off, nrows * GRID_W), :]

    def slab_start(pi):
        return clip(2 * pi - NA_ROWS_MAX // 2, 0, rows - NA_SLAB)

    def scores(pi, slot):
        ws = slab_start(pi)
        kw = rows_at(k_ref, ws, NA_SLAB)[...]
        q = rows_at(q_ref, 2 * pi, 2)[...]
        zero = jnp.zeros_like(q)
        first2 = jnp.concatenate([first, first], axis=0)
        qa = jnp.where(first2, q, zero)
        qb = jnp.where(first2, zero, q)
        q4 = jnp.concatenate([qa[:GRID_W], qb[:GRID_W], qa[GRID_W:], qb[GRID_W:]], axis=0)
        s = lax.dot_general(kw, q4, _NT, preferred_element_type=F32)
        mx = []
        for rq in range(2):
            qrow = 2 * pi + rq
            wq = clip(qrow - NA_ROWS_MAX // 2, 0, rows - NA_ROWS_MAX)
            m = None
            for i in range(NA_SLAB):
                krow = ws + i
                valid = (krow >= wq) & (krow < wq + NA_ROWS_MAX)
                dr = krow - qrow + (NA_ROWS_MAX - 1)
                if isinstance(valid, bool):
                    idx = dr if valid else NA_MASKED
                else:
                    idx = jnp.where(valid, dr, NA_MASKED)
                blk = (s[i * GRID_W:(i + 1) * GRID_W, rq * LANES:(rq + 1) * LANES]
                       + t_ref[0, idx])
                s_ref[slot, i * GRID_W:(i + 1) * GRID_W, rq * LANES:(rq + 1) * LANES] = blk
                bm = jnp.max(blk, axis=0, keepdims=True)
                m = bm if m is None else jnp.maximum(m, bm)
            mx.append(m)
        return jnp.concatenate(mx, axis=1)

    def values(pi, pslot):
        prev = max(pi - 1, 0) if isinstance(pi, int) else jnp.maximum(pi - 1, 0)
        vw = rows_at(v_ref, slab_start(prev), NA_SLAB)[...]
        ot = lax.dot_general(vw, p_ref[pslot], _TN, preferred_element_type=F32) / l_ref[pslot]
        o = ot.T
        out = jnp.concatenate(
            [jnp.where(first, o[:GRID_W], o[GRID_W:pair_w]),
             jnp.where(first, o[pair_w:pair_w + GRID_W], o[pair_w + GRID_W:])], axis=0)
        rows_at(o_ref, 2 * prev, 2)[...] = out.astype(o_ref.dtype)

    def trip(pi, carry, slot=0, last=False):
        mx = carry
        mx_next = mx if last else scores(pi + 1, 1 - slot)
        values(pi, 1 - slot)
        p = jnp.exp2(s_ref[slot] - mx)
        p_ref[slot] = p.astype(BF16)
        l_ref[slot] = jnp.sum(p, axis=0, keepdims=True)
        return mx_next

    p_ref[...] = jnp.zeros_like(p_ref)
    l_ref[...] = jnp.ones_like(l_ref)
    carry = scores(0, 0)
    looped = 2 * ((npair - 1) // 2)
    carry = lax.fori_loop(0, looped // 2,
                          lambda jj, c: trip(2 * jj + 1, trip(2 * jj, c, 0), 1), carry)
    for pi in range(looped, npair - 1):
        carry = trip(pi, carry, pi % 2)
    trip(npair - 1, carry, (npair - 1) % 2, last=True)
    values(npair, (npair - 1) % 2)


def _na_bias_tables(rpb):
    c = jnp.arange(GRID_W)
    col_start = jnp.clip(c - NA_COLS // 2, 0, GRID_W - NA_COLS)
    in_win = (c[None, :] >= col_start[:, None]) & (c[None, :] < col_start[:, None] + NA_COLS)
    dc = jnp.clip(c[None, :] - c[:, None], -(NA_COLS - 1), NA_COLS - 1) + (NA_COLS - 1)
    tm = jnp.where(in_win[None, None], rpb[:, :, dc] * LOG2E, NEG_BIG)
    nd = tm.shape[1]
    tt = tm.reshape(NA_HEADS // 2, 2, nd, GRID_W, GRID_W).transpose(0, 2, 4, 1, 3)
    tt = tt.reshape(NA_HEADS // 2, nd, GRID_W, 2 * GRID_W)
    masked = jnp.full((NA_HEADS // 2, NA_MASKED + 1 - nd, GRID_W, 2 * GRID_W), NEG_BIG, F32)
    return jnp.concatenate([tt, masked], axis=1)


def _neigh_attention(z, tables, batch, seq):
    n = z.shape[0]
    slab = NA_SLAB * GRID_W
    kern = functools.partial(_na_kernel, rows=seq // GRID_W)

    def column(col):
        return pl.BlockSpec((seq, LANES), lambda b, h: (b, col + h))

    return pl.pallas_call(
        kern,
        grid=(batch, NA_HEADS // 2),
        in_specs=[column(COL_QN), column(COL_KN), column(COL_VN),
                  pl.BlockSpec((1,) + tables.shape[1:], lambda b, h: (h, 0, 0, 0))],
        out_specs=column(0),
        out_shape=jax.ShapeDtypeStruct((n, NA_HEADS * NA_HEAD_DIM), BF16),
        scratch_shapes=[pltpu.VMEM((2, slab, 2 * LANES), F32), pltpu.VMEM((2, slab, 2 * LANES), BF16),
                        pltpu.VMEM((2, 1, 2 * LANES), F32)],
        compiler_params=_params("parallel", "parallel"),
        name="neigh_attention",
    )(z, z, z, tables)


def _rope_tables(seq):
    t = jnp.arange(seq)
    row = (t // GRID_W).astype(F32)
    col = (t % GRID_W).astype(F32)
    half = GQ_HEAD_DIM // 2
    freqs = ROPE_THETA ** (-jnp.arange(0, half, 2, dtype=F32) / half)
    ang = jnp.concatenate([row[:, None] * freqs, col[:, None] * freqs], axis=-1)
    cos = jnp.repeat(jnp.cos(ang), 2, axis=-1)
    sin = jnp.repeat(jnp.sin(ang), 2, axis=-1)
    sign = jnp.where(jnp.arange(GQ_HEAD_DIM) % 2 == 0, -1.0, 1.0).astype(F32)
    return cos, sin * sign


def _qk_rope_kernel(q_ref, k_ref, cos_ref, sin_ref, gq_ref, gk_ref, qo_ref, ko_ref):
    cos = cos_ref[...]
    sin = sin_ref[...]
    even = lax.broadcasted_iota(jnp.int32, cos.shape, 1) % 2 == 0

    def rope(x, g):
        y = _rms(x.astype(F32), g)
        swapped = jnp.where(even, pltpu.roll(y, LANES - 1, 1), pltpu.roll(y, 1, 1))
        return y * cos + swapped * sin

    scale = GQ_HEAD_DIM ** -0.5 * LOG2E
    for hd in range(GQ_HEADS):
        sl = slice(hd * LANES, (hd + 1) * LANES)
        qo_ref[:, sl] = (rope(q_ref[:, sl], gq_ref[...]) * scale).astype(qo_ref.dtype)
    for hd in range(GQ_KV_HEADS):
        sl = slice(hd * LANES, (hd + 1) * LANES)
        ko_ref[:, sl] = rope(k_ref[:, sl], gk_ref[...]).astype(ko_ref.dtype)


def _qk_rope(z, cos, sin, gq, gk, seq, tm=512):
    n = z.shape[0]
    nper = seq // tm
    qw = GQ_HEADS * LANES
    kw = GQ_KV_HEADS * LANES
    return pl.pallas_call(
        _qk_rope_kernel,
        grid=(n // tm,),
        in_specs=[
            pl.BlockSpec((tm, qw), lambda i: (i, COL_QC * LANES // qw)),
            pl.BlockSpec((tm, kw), lambda i: (i, COL_KC * LANES // kw)),
            pl.BlockSpec((tm, LANES), lambda i: (i % nper, 0)),
            pl.BlockSpec((tm, LANES), lambda i: (i % nper, 0)),
            pl.BlockSpec((1, LANES), lambda i: (0, 0)),
            pl.BlockSpec((1, LANES), lambda i: (0, 0)),
        ],
        out_specs=[
            pl.BlockSpec((tm, qw), lambda i: (i, 0)),
            pl.BlockSpec((tm, kw), lambda i: (i, 0)),
        ],
        out_shape=[jax.ShapeDtypeStruct((n, qw), BF16), jax.ShapeDtypeStruct((n, kw), BF16)],
        compiler_params=_params("parallel"),
        name="qk_norm_rope",
    )(z, z, cos, sin, gq, gk)


def _gqa_kernel(q_ref, k_ref, v_ref, o_ref, vt_ref, s_ref, p_ref, a_ref, *, seq, tq, tk):
    group = GQ_HEADS // GQ_KV_HEADS
    q = jnp.concatenate([q_ref[:, g * LANES:(g + 1) * LANES] for g in range(group)], axis=0)
    _transpose_values(v_ref, vt_ref, seq)
    (l,) = _flash_transposed([q], k_ref, vt_ref, (s_ref,), (p_ref,), (a_ref,), seq, tk)
    o = (a_ref[...] / l).T
    for g in range(group):
        o_ref[:, g * LANES:(g + 1) * LANES] = o[g * tq:(g + 1) * tq].astype(o_ref.dtype)


def _gqa_attention(qr, kr, z, batch, seq, tq=512, tk=512):
    n = z.shape[0]
    nq = seq // tq
    group = GQ_HEADS // GQ_KV_HEADS
    gw = group * LANES
    kern = functools.partial(_gqa_kernel, seq=seq, tq=tq, tk=tk)
    return pl.pallas_call(
        kern,
        grid=(batch, GQ_KV_HEADS, nq),
        in_specs=[
            pl.BlockSpec((tq, gw), lambda b, h, i: (b * nq + i, h)),
            pl.BlockSpec((seq, LANES), lambda b, h, i: (b, h)),
            pl.BlockSpec((seq, LANES), lambda b, h, i: (b, COL_VC + h)),
        ],
        out_specs=pl.BlockSpec((tq, gw), lambda b, h, i: (b * nq + i, h)),
        scratch_shapes=[pltpu.VMEM((LANES, seq), BF16),
                        pltpu.VMEM((2, tk, group * tq), F32), pltpu.VMEM((2, tk, group * tq), BF16),
                        pltpu.VMEM((LANES, group * tq), F32)],
        out_shape=jax.ShapeDtypeStruct((n, GQ_HEADS * LANES), BF16),
        compiler_params=_params("parallel", "parallel", "arbitrary"),
        name="gqa_attention",
    )(qr, kr, z)


def _merge_kernel(h_ref, oa_ref, ob_ref, oc_ref, g0_ref, g1_ref,
                  wa_ref, wb_ref, wc_ref, wo_ref, gp_ref, o_ref):
    d = D_MODEL
    g0 = g0_ref[...]
    g1 = g1_ref[...]
    gate_a = jax.nn.sigmoid(g0[:, :d].astype(F32))
    gate_b = jax.nn.sigmoid(jnp.concatenate([g0[:, d:], g1[:, :2 * d - g0.shape[1]]], axis=1).astype(F32))
    gate_c = jax.nn.sigmoid(g1[:, 2 * d - g0.shape[1]:].astype(F32))
    merged = gate_a * jnp.dot(oa_ref[...], wa_ref[...], preferred_element_type=F32)
    merged += gate_b * jnp.dot(ob_ref[...], wb_ref[...], preferred_element_type=F32)
    merged += gate_c * jnp.dot(oc_ref[...], wc_ref[...], preferred_element_type=F32)
    y = jnp.dot(merged.astype(BF16), wo_ref[...], preferred_element_type=F32)
    o_ref[...] = h_ref[...] + _rms(y, gp_ref[...])


def _merge(h, oa, ob, oc, z, wa, wb, wc, wo, gp, tm=512):
    n, d = h.shape
    gw = 3 * d // 2
    gblk = COL_GZ * LANES // gw
    row = lambda i: (i, 0)
    const = lambda i: (0, 0)
    return pl.pallas_call(
        _merge_kernel,
        grid=(n // tm,),
        in_specs=[
            pl.BlockSpec((tm, d), row),
            pl.BlockSpec((tm, d), row),
            pl.BlockSpec((tm, d), row),
            pl.BlockSpec((tm, d), row),
            pl.BlockSpec((tm, gw), lambda i: (i, gblk)),
            pl.BlockSpec((tm, gw), lambda i: (i, gblk + 1)),
            pl.BlockSpec((d, d), const),
            pl.BlockSpec((d, d), const),
            pl.BlockSpec((d, d), const),
            pl.BlockSpec((d, d), const),
            pl.BlockSpec((1, d), const),
        ],
        out_specs=pl.BlockSpec((tm, d), row),
        out_shape=jax.ShapeDtypeStruct((n, d), F32),
        compiler_params=_params("parallel"),
        name="gated_merge",
    )(h, oa, ob, oc, z, z, wa, wb, wc, wo, gp)


def _mlp_kernel(h_ref, g1_ref, wu_ref, wd_ref, g2_ref, o_ref, u_ref, acc_ref):
    j = pl.program_id(1)

    @pl.when(j == 0)
    def _():
        u_ref[...] = _rms(h_ref[...], g1_ref[...]).astype(BF16)
        acc_ref[...] = jnp.zeros_like(acc_ref)

    hid = jnp.dot(u_ref[...], wu_ref[...], preferred_element_type=F32)
    hid = jnp.square(jnp.maximum(hid, 0.0)).astype(BF16)
    acc_ref[...] += jnp.dot(hid, wd_ref[...], preferred_element_type=F32)

    @pl.when(j == pl.num_programs(1) - 1)
    def _():
        o_ref[...] = h_ref[...] + _rms(acc_ref[...], g2_ref[...])


def _mlp(h, g1, wu, wd, g2, tm=1024, tf=1024):
    n, d = h.shape
    dff = wu.shape[1]
    return pl.pallas_call(
        _mlp_kernel,
        grid=(n // tm, dff // tf),
        in_specs=[
            pl.BlockSpec((tm, d), lambda i, j: (i, 0)),
            pl.BlockSpec((1, d), lambda i, j: (0, 0)),
            pl.BlockSpec((d, tf), lambda i, j: (0, j)),
            pl.BlockSpec((tf, d), lambda i, j: (j, 0)),
            pl.BlockSpec((1, d), lambda i, j: (0, 0)),
        ],
        out_specs=pl.BlockSpec((tm, d), lambda i, j: (i, 0)),
        out_shape=jax.ShapeDtypeStruct((n, d), F32),
        scratch_shapes=[pltpu.VMEM((tm, d), BF16), pltpu.VMEM((tm, d), F32)],
        compiler_params=_params("parallel", "arbitrary"),
        name="relu2_mlp",
    )(h, g1, wu, wd, g2)


def _ple_kernel(h_ref, p_ref, wp_ref, wg_ref, g_ref, o_ref):
    h = h_ref[...]
    e = jnp.dot(p_ref[...].astype(BF16), wp_ref[...], preferred_element_type=F32)
    gate = jax.nn.sigmoid(jnp.dot(h.astype(BF16), wg_ref[...], preferred_element_type=F32))
    o_ref[...] = h + _rms(e * gate, g_ref[...])


def _ple(h, p, wp, wg, g, tm=1024):
    n, d = h.shape
    pd = p.shape[1]
    return pl.pallas_call(
        _ple_kernel,
        grid=(n // tm,),
        in_specs=[
            pl.BlockSpec((tm, d), lambda i: (i, 0)),
            pl.BlockSpec((tm, pd), lambda i: (i, 0)),
            pl.BlockSpec((pd, d), lambda i: (0, 0)),
            pl.BlockSpec((d, d), lambda i: (0, 0)),
            pl.BlockSpec((1, d), lambda i: (0, 0)),
        ],
        out_specs=pl.BlockSpec((tm, d), lambda i: (i, 0)),
        out_shape=jax.ShapeDtypeStruct((n, d), F32),
        compiler_params=_params("parallel"),
        name="gated_ple",
    )(h, p, wp, wg, g)


def _trunk(x, ple, w):
    batch, seq, d = x.shape
    n = batch * seq
    h = x.reshape(n, d)
    cos, sin = _rope_tables(seq)
    slopes = jnp.exp2(-8.0 * jnp.arange(1, DA_HEADS + 1, dtype=F32) / DA_HEADS)
    col_scale = _in_proj_col_scale()
    for i in range(DEPTH):
        lambda_init = 0.8 - 0.6 * math.exp(-0.3 * i)
        z = _norm_matmul(h, w["g_pre_mix"][i][None], w["w_in"][i], col_scale)
        oa = _diff_attention(z, slopes, w["da_lambda"][i], w["da_norm"][i][None],
                             batch, seq, lambda_init)
        ob = _neigh_attention(z, w["na_tables"][i], batch, seq)
        qr, kr = _qk_rope(z, cos, sin, w["gq_q_norm"][i][None], w["gq_k_norm"][i][None], seq)
        oc = _gqa_attention(qr, kr, z, batch, seq)
        h = _merge(h, oa, ob, oc, z, w["w_br_a"][i], w["w_br_b"][i], w["w_br_c"][i],
                   w["w_o"][i], w["g_post_mix"][i][None])
        h = _mlp(h, w["g_pre_mlp"][i][None], w["w_up"][i], w["w_down"][i],
                 w["g_post_mlp"][i][None])
        h = _ple(h, ple[i].reshape(n, PLE_DIM), w["w_ple"][i], w["w_ple_gate"][i],
                 w["g_ple"][i][None])
    return h.reshape(batch, seq, d)


def kernel(x_prompt, x_sample, p_prompt, p_sample, w_in, da_lambda, da_norm, na_rpb, gq_q_norm, gq_k_norm, w_br_a, w_br_b, w_br_c, w_o, g_pre_mix, g_post_mix, g_pre_mlp, g_post_mlp, w_up, w_down, w_ple, w_ple_gate, g_ple):
    w = dict(
        w_in=w_in.astype(BF16), da_lambda=da_lambda, da_norm=da_norm,
        na_tables=jax.vmap(_na_bias_tables)(na_rpb),
        gq_q_norm=gq_q_norm, gq_k_norm=gq_k_norm,
        w_br_a=w_br_a.astype(BF16), w_br_b=w_br_b.astype(BF16), w_br_c=w_br_c.astype(BF16),
        w_o=w_o.astype(BF16), g_pre_mix=g_pre_mix, g_post_mix=g_post_mix,
        g_pre_mlp=g_pre_mlp, g_post_mlp=g_post_mlp,
        w_up=w_up.astype(BF16), w_down=w_down.astype(BF16),
        w_ple=w_ple.astype(BF16), w_ple_gate=w_ple_gate.astype(BF16), g_ple=g_ple,
    )
    return _trunk(x_prompt, p_prompt, w), _trunk(x_sample, p_sample, w)
```

```python
import functools
import math

import jax
import jax.numpy as jnp
from jax import lax
from jax.experimental import pallas as pl
from jax.experimental.pallas import tpu as pltpu

F32 = jnp.float32
BF16 = jnp.bfloat16

D_MODEL = 1024
DEPTH = 4
GRID_W = 64
EPS = 1e-6
DA_HEADS = 8
DA_HEAD_DIM = 64
NA_HEADS = 16
NA_HEAD_DIM = 64
NA_ROWS_MAX = 8
NA_COLS = 16
GQ_HEADS = 8
GQ_KV_HEADS = 2
GQ_HEAD_DIM = 128
ROPE_THETA = 10000.0
D_FF = 4 * D_MODEL
PLE_DIM = 256
IN_W = 10752

LANES = 128
COL_QA, COL_KA, COL_VA = 0, 8, 16
COL_QN, COL_KN, COL_VN = 24, 32, 40
COL_QC, COL_KC, COL_VC = 48, 56, 58
COL_GZ = 60
NEG_BIG = -1e30
LOG2E = math.log2(math.e)

VMEM_LIMIT = 48 * 1024 * 1024


def _params(*sem):
    return pltpu.CompilerParams(dimension_semantics=sem, vmem_limit_bytes=VMEM_LIMIT)


def _rms(x, g):
    return x * lax.rsqrt(jnp.mean(x * x, axis=-1, keepdims=True) + EPS) * g


def _norm_matmul_kernel(x_ref, g_ref, w_ref, cs_ref, o_ref, u_ref):
    @pl.when(pl.program_id(1) == 0)
    def _():
        u_ref[...] = _rms(x_ref[...], g_ref[...]).astype(BF16)

    acc = jnp.dot(u_ref[...], w_ref[...], preferred_element_type=F32)
    o_ref[...] = (acc * cs_ref[...]).astype(o_ref.dtype)


def _in_proj_col_scale():
    cs = jnp.ones((IN_W,), F32)
    cs = cs.at[COL_QA * LANES:COL_KA * LANES].set(DA_HEAD_DIM ** -0.5 * LOG2E)
    cs = cs.at[COL_QN * LANES:COL_KN * LANES].set(NA_HEAD_DIM ** -0.5 * LOG2E)
    return cs[None]


def _norm_matmul(h, g, w, cs, tm=1024, tn=1536):
    n, d = h.shape
    width = w.shape[1]
    return pl.pallas_call(
        _norm_matmul_kernel,
        grid=(n // tm, width // tn),
        in_specs=[
            pl.BlockSpec((tm, d), lambda i, j: (i, 0)),
            pl.BlockSpec((1, d), lambda i, j: (0, 0)),
            pl.BlockSpec((d, tn), lambda i, j: (0, j)),
            pl.BlockSpec((1, tn), lambda i, j: (0, j)),
        ],
        out_specs=pl.BlockSpec((tm, tn), lambda i, j: (i, j)),
        out_shape=jax.ShapeDtypeStruct((n, width), BF16),
        scratch_shapes=[pltpu.VMEM((tm, d), BF16)],
        compiler_params=_params("parallel", "arbitrary"),
        name="norm_in_proj",
    )(h, g, w, cs)


_NT = (((1,), (1,)), ((), ()))
_TN = (((0,), (0,)), ((), ()))


def _transpose_values(v_ref, vt_ref, seq, chunk=512):
    @pl.when(pl.program_id(2) == 0)
    def _():
        for c in range(seq // chunk):
            vt_ref[:, c * chunk:(c + 1) * chunk] = v_ref[c * chunk:(c + 1) * chunk, :].T


def _flash_transposed(q_maps, k_ref, vt_ref, s_refs, p_refs, a_refs, seq, tk,
                      bias_fn=None, shift_fn=None, double=True):
    nmaps = len(q_maps)
    nq = q_maps[0].shape[0]
    nk = seq // tk

    def start(c):
        return c * tk if isinstance(c, int) else pl.multiple_of(c * tk, tk)

    def shift(c):
        return 0.0 if shift_fn is None else shift_fn(c)

    def scores(kc, slot):
        k = k_ref[pl.ds(start(kc), tk), :]
        bias = None if bias_fn is None else bias_fn(kc)
        mx = []
        for j in range(nmaps):
            s = lax.dot_general(k, q_maps[j], _NT, preferred_element_type=F32)
            if bias is not None:
                s = s - bias
            s_refs[j][slot] = s
            mx.append(jnp.max(s, axis=0, keepdims=True) - shift(kc))
        return tuple(mx)

    def values(i, pslot):
        prev = max(i - 1, 0) if isinstance(i, int) else jnp.maximum(i - 1, 0)
        vt_prev = vt_ref[:, pl.ds(start(prev), tk)]
        return [jnp.dot(vt_prev, p_refs[j][pslot], preferred_element_type=F32) for j in range(nmaps)]

    def softmax(i, slot, carry, pvs):
        ms, ls, mxs = carry
        m_new, l_new = [], []
        for j in range(nmaps):
            if pvs is None:
                mn = mxs[j]
                p = jnp.exp2(s_refs[j][slot] - (mn + shift(i)))
                l_new.append(jnp.sum(p, axis=0, keepdims=True))
                a_refs[j][...] = jnp.zeros_like(a_refs[j])
            else:
                mn = jnp.maximum(ms[j], mxs[j])
                alpha = jnp.exp2(ms[j] - mn)
                p = jnp.exp2(s_refs[j][slot] - (mn + shift(i)))
                l_new.append(alpha * ls[j] + jnp.sum(p, axis=0, keepdims=True))
                a_refs[j][...] = alpha * (a_refs[j][...] + pvs[j])
            m_new.append(mn)
            p_refs[j][slot] = p.astype(BF16)
        return tuple(m_new), tuple(l_new)

    def trip(i, carry, slot=0, first=False, last=False):
        if double:
            mx_next = carry[2] if last else scores(i + 1, 1 - slot)
            m_new, l_new = softmax(i, slot, carry, None if first else values(i, 1 - slot))
        else:
            m_new, l_new = softmax(i, 0, carry, None if first else values(i, 0))
            mx_next = carry[2] if last else scores(i + 1, 0)
        return m_new, l_new, mx_next

    carry = trip(0, (None, None, scores(0, 0)), 0, first=True, last=nk == 1)
    last_slot = 0
    if nk > 1:
        if double:
            done = 1 + 2 * ((nk - 2) // 2)
            carry = lax.fori_loop(
                0, (done - 1) // 2, lambda jj, c: trip(2 * jj + 2, trip(2 * jj + 1, c, 1), 0), carry)
        else:
            done = nk - 1
            carry = lax.fori_loop(1, done, trip, carry)
        for i in range(done, nk - 1):
            carry = trip(i, carry, i % 2)
        last_slot = (nk - 1) % 2 if double else 0
        carry = trip(nk - 1, carry, last_slot, last=True)
    ls = carry[1]
    vt_last = vt_ref[:, (nk - 1) * tk:]
    for j in range(nmaps):
        a_refs[j][...] += jnp.dot(vt_last, p_refs[j][last_slot], preferred_element_type=F32)
    return ls


def _da_kernel(slopes_ref, q_ref, k_ref, v_ref, lam_ref, gn_ref, o_ref,
               vt_ref, bm_ref, s1_ref, s2_ref, p1_ref, p2_ref, a1_ref, a2_ref,
               *, seq, tq, tk, lambda_init, double):
    assert tq == tk
    h = pl.program_id(1)
    q0 = pl.program_id(2) * tq
    slope = slopes_ref[h] * LOG2E
    q = q_ref[...]
    lane = lax.broadcasted_iota(jnp.int32, (tq, LANES), 1)
    zero = jnp.zeros_like(q)
    q1 = jnp.where(lane < DA_HEAD_DIM, q, zero)
    q2 = jnp.where(lane >= DA_HEAD_DIM, q, zero)

    @pl.when(pl.program_id(2) == 0)
    def _():
        srel = slope * (lax.broadcasted_iota(jnp.int32, (tk, tq), 1)
                        - lax.broadcasted_iota(jnp.int32, (tk, tq), 0)).astype(F32)
        bm_ref[0] = srel
        bm_ref[1] = jnp.abs(srel)
        bm_ref[2] = -srel

    def alibi_tile(kc):
        k0 = kc * tk
        return bm_ref[jnp.where(k0 < q0, 0, jnp.where(k0 == q0, 1, 2))]

    def alibi_shift(kc):
        return slope * jnp.abs((q0 - kc * tk).astype(F32))

    _transpose_values(v_ref, vt_ref, seq)
    l1, l2 = _flash_transposed([q1, q2], k_ref, vt_ref, (s1_ref, s2_ref), (p1_ref, p2_ref),
                               (a1_ref, a2_ref), seq, tk, bias_fn=alibi_tile,
                               shift_fn=alibi_shift, double=double)
    lm = lam_ref[...]
    lam = (jnp.exp(jnp.sum(lm[0:1] * lm[1:2], axis=-1, keepdims=True))
           - jnp.exp(jnp.sum(lm[2:3] * lm[3:4], axis=-1, keepdims=True)) + lambda_init)
    o = (a1_ref[...] / l1 - lam * (a2_ref[...] / l2)).T
    o_ref[...] = (_rms(o, gn_ref[...]) * (1.0 - lambda_init)).astype(o_ref.dtype)


DA_MAX_TILE = 1024


def _diff_attention(z, slopes, lam_p, gn, batch, seq, lambda_init, tile=None, double=False):
    n = z.shape[0]
    if tile is None:
        tile = min(DA_MAX_TILE, seq // 4)
    tq = tk = tile
    nq = seq // tq
    kern = functools.partial(_da_kernel, seq=seq, tq=tq, tk=tk, lambda_init=lambda_init,
                             double=double)
    nslot = 2 if double else 1
    return pl.pallas_call(
        kern,
        grid_spec=pltpu.PrefetchScalarGridSpec(
            num_scalar_prefetch=1,
            grid=(batch, DA_HEADS, nq),
            in_specs=[
                pl.BlockSpec((tq, LANES), lambda b, h, i, s: (b * nq + i, COL_QA + h)),
                pl.BlockSpec((seq, LANES), lambda b, h, i, s: (b, COL_KA + h)),
                pl.BlockSpec((seq, LANES), lambda b, h, i, s: (b, COL_VA + h)),
                pl.BlockSpec((4, DA_HEAD_DIM), lambda b, h, i, s: (0, 0)),
                pl.BlockSpec((1, LANES), lambda b, h, i, s: (0, 0)),
            ],
            out_specs=pl.BlockSpec((tq, LANES), lambda b, h, i, s: (b * nq + i, h)),
            scratch_shapes=[pltpu.VMEM((LANES, seq), BF16), pltpu.VMEM((3, tk, tq), F32),
                            pltpu.VMEM((nslot, tk, tq), F32), pltpu.VMEM((nslot, tk, tq), F32),
                            pltpu.VMEM((nslot, tk, tq), BF16), pltpu.VMEM((nslot, tk, tq), BF16),
                            pltpu.VMEM((LANES, tq), F32), pltpu.VMEM((LANES, tq), F32)],
        ),
        out_shape=jax.ShapeDtypeStruct((n, DA_HEADS * LANES), BF16),
        compiler_params=_params("parallel", "parallel", "arbitrary"),
        name="diff_attention",
    )(slopes, z, z, z, lam_p, gn)


NA_SLAB = NA_ROWS_MAX + 1
NA_MASKED = 2 * NA_ROWS_MAX - 1


def _na_kernel(q_ref, k_ref, v_ref, t_ref, o_ref, s_ref, p_ref, l_ref, *, rows):
    npair = rows // 2
    slab = NA_SLAB * GRID_W
    pair_w = 2 * GRID_W
    first = lax.broadcasted_iota(jnp.int32, (GRID_W, LANES), 1) < NA_HEAD_DIM

    def clip(x, lo, hi):
        return min(max(x, lo), hi) if isinstance(x, int) else jnp.clip(x, lo, hi)

    def rows_at(ref, row, nrows):
        off = row * GRID_W
        if not isinstance(off, int):
            off = pl.multiple_of(off, GRID_W)
        return ref.at[pl.ds(off, nrows * GRID_W), :]

    def slab_start(pi):
        return clip(2 * pi - NA_ROWS_MAX // 2, 0, rows - NA_SLAB)

    def scores(pi, slot):
        ws = slab_start(pi)
        kw = rows_at(k_ref, ws, NA_SLAB)[...]
        q = rows_at(q_ref, 2 * pi, 2)[...]
        zero = jnp.zeros_like(q)
        first2 = jnp.concatenate([first, first], axis=0)
        qa = jnp.where(first2, q, zero)
        qb = jnp.where(first2, zero, q)
        q4 = jnp.concatenate([qa[:GRID_W], qb[:GRID_W], qa[GRID_W:], qb[GRID_W:]], axis=0)
        s = lax.dot_general(kw, q4, _NT, preferred_element_type=F32)
        mx = []
        for rq in range(2):
            qrow = 2 * pi + rq
            wq = clip(qrow - NA_ROWS_MAX // 2, 0, rows - NA_ROWS_MAX)
            m = None
            for i in range(NA_SLAB):
                krow = ws + i
                valid = (krow >= wq) & (krow < wq + NA_ROWS_MAX)
                dr = krow - qrow + (NA_ROWS_MAX - 1)
                if isinstance(valid, bool):
                    idx = dr if valid else NA_MASKED
                else:
                    idx = jnp.where(valid, dr, NA_MASKED)
                blk = (s[i * GRID_W:(i + 1) * GRID_W, rq * LANES:(rq + 1) * LANES]
                       + t_ref[0, idx])
                s_ref[slot, i * GRID_W:(i + 1) * GRID_W, rq * LANES:(rq + 1) * LANES] = blk
                bm = jnp.max(blk, axis=0, keepdims=True)
                m = bm if m is None else jnp.maximum(m, bm)
            mx.append(m)
        return jnp.concatenate(mx, axis=1)

    def values(pi, pslot):
        prev = max(pi - 1, 0) if isinstance(pi, int) else jnp.maximum(pi - 1, 0)
        vw = rows_at(v_ref, slab_start(prev), NA_SLAB)[...]
        ot = lax.dot_general(vw, p_ref[pslot], _TN, preferred_element_type=F32) / l_ref[pslot]
        o = ot.T
        out = jnp.concatenate(
            [jnp.where(first, o[:GRID_W], o[GRID_W:pair_w]),
             jnp.where(first, o[pair_w:pair_w + GRID_W], o[pair_w + GRID_W:])], axis=0)
        rows_at(o_ref, 2 * prev, 2)[...] = out.astype(o_ref.dtype)

    def trip(pi, carry, slot=0, last=False):
        mx = carry
        mx_next = mx if last else scores(pi + 1, 1 - slot)
        values(pi, 1 - slot)
        p = jnp.exp2(s_ref[slot] - mx)
        p_ref[slot] = p.astype(BF16)
        l_ref[slot] = jnp.sum(p, axis=0, keepdims=True)
        return mx_next

    p_ref[...] = jnp.zeros_like(p_ref)
    l_ref[...] = jnp.ones_like(l_ref)
    carry = scores(0, 0)
    looped = 2 * ((npair - 1) // 2)
    carry = lax.fori_loop(0, looped // 2,
                          lambda jj, c: trip(2 * jj + 1, trip(2 * jj, c, 0), 1), carry)
    for pi in range(looped, npair - 1):
        carry = trip(pi, carry, pi % 2)
    trip(npair - 1, carry, (npair - 1) % 2, last=True)
    values(npair, (npair - 1) % 2)


def _na_bias_tables(rpb):
    c = jnp.arange(GRID_W)
    col_start = jnp.clip(c - NA_COLS // 2, 0, GRID_W - NA_COLS)
    in_win = (c[None, :] >= col_start[:, None]) & (c[None, :] < col_start[:, None] + NA_COLS)
    dc = jnp.clip(c[None, :] - c[:, None], -(NA_COLS - 1), NA_COLS - 1) + (NA_COLS - 1)
    tm = jnp.where(in_win[None, None], rpb[:, :, dc] * LOG2E, NEG_BIG)
    nd = tm.shape[1]
    tt = tm.reshape(NA_HEADS // 2, 2, nd, GRID_W, GRID_W).transpose(0, 2, 4, 1, 3)
    tt = tt.reshape(NA_HEADS // 2, nd, GRID_W, 2 * GRID_W)
    masked = jnp.full((NA_HEADS // 2, NA_MASKED + 1 - nd, GRID_W, 2 * GRID_W), NEG_BIG, F32)
    return jnp.concatenate([tt, masked], axis=1)


def _neigh_attention(z, tables, batch, seq):
    n = z.shape[0]
    slab = NA_SLAB * GRID_W
    kern = functools.partial(_na_kernel, rows=seq // GRID_W)

    def column(col):
        return pl.BlockSpec((seq, LANES), lambda b, h: (b, col + h))

    return pl.pallas_call(
        kern,
        grid=(batch, NA_HEADS // 2),
        in_specs=[column(COL_QN), column(COL_KN), column(COL_VN),
                  pl.BlockSpec((1,) + tables.shape[1:], lambda b, h: (h, 0, 0, 0))],
        out_specs=column(0),
        out_shape=jax.ShapeDtypeStruct((n, NA_HEADS * NA_HEAD_DIM), BF16),
        scratch_shapes=[pltpu.VMEM((2, slab, 2 * LANES), F32), pltpu.VMEM((2, slab, 2 * LANES), BF16),
                        pltpu.VMEM((2, 1, 2 * LANES), F32)],
        compiler_params=_params("parallel", "parallel"),
        name="neigh_attention",
    )(z, z, z, tables)


def _rope_tables(seq):
    t = jnp.arange(seq)
    row = (t // GRID_W).astype(F32)
    col = (t % GRID_W).astype(F32)
    half = GQ_HEAD_DIM // 2
    freqs = ROPE_THETA ** (-jnp.arange(0, half, 2, dtype=F32) / half)
    ang = jnp.concatenate([row[:, None] * freqs, col[:, None] * freqs], axis=-1)
    cos = jnp.repeat(jnp.cos(ang), 2, axis=-1)
    sin = jnp.repeat(jnp.sin(ang), 2, axis=-1)
    sign = jnp.where(jnp.arange(GQ_HEAD_DIM) % 2 == 0, -1.0, 1.0).astype(F32)
    return cos, sin * sign


def _qk_rope_kernel(q_ref, k_ref, cos_ref, sin_ref, gq_ref, gk_ref, qo_ref, ko_ref):
    cos = cos_ref[...]
    sin = sin_ref[...]
    even = lax.broadcasted_iota(jnp.int32, cos.shape, 1) % 2 == 0

    def rope(x, g):
        y = _rms(x.astype(F32), g)
        swapped = jnp.where(even, pltpu.roll(y, LANES - 1, 1), pltpu.roll(y, 1, 1))
        return y * cos + swapped * sin

    scale = GQ_HEAD_DIM ** -0.5 * LOG2E
    for hd in range(GQ_HEADS):
        sl = slice(hd * LANES, (hd + 1) * LANES)
        qo_ref[:, sl] = (rope(q_ref[:, sl], gq_ref[...]) * scale).astype(qo_ref.dtype)
    for hd in range(GQ_KV_HEADS):
        sl = slice(hd * LANES, (hd + 1) * LANES)
        ko_ref[:, sl] = rope(k_ref[:, sl], gk_ref[...]).astype(ko_ref.dtype)


def _qk_rope(z, cos, sin, gq, gk, seq, tm=512):
    n = z.shape[0]
    nper = seq // tm
    qw = GQ_HEADS * LANES
    kw = GQ_KV_HEADS * LANES
    return pl.pallas_call(
        _qk_rope_kernel,
        grid=(n // tm,),
        in_specs=[
            pl.BlockSpec((tm, qw), lambda i: (i, COL_QC * LANES // qw)),
            pl.BlockSpec((tm, kw), lambda i: (i, COL_KC * LANES // kw)),
            pl.BlockSpec((tm, LANES), lambda i: (i % nper, 0)),
            pl.BlockSpec((tm, LANES), lambda i: (i % nper, 0)),
            pl.BlockSpec((1, LANES), lambda i: (0, 0)),
            pl.BlockSpec((1, LANES), lambda i: (0, 0)),
        ],
        out_specs=[
            pl.BlockSpec((tm, qw), lambda i: (i, 0)),
            pl.BlockSpec((tm, kw), lambda i: (i, 0)),
        ],
        out_shape=[jax.ShapeDtypeStruct((n, qw), BF16), jax.ShapeDtypeStruct((n, kw), BF16)],
        compiler_params=_params("parallel"),
        name="qk_norm_rope",
    )(z, z, cos, sin, gq, gk)


def _gqa_kernel(q_ref, k_ref, v_ref, o_ref, vt_ref, s_ref, p_ref, a_ref, *, seq, tq, tk):
    group = GQ_HEADS // GQ_KV_HEADS
    q = jnp.concatenate([q_ref[:, g * LANES:(g + 1) * LANES] for g in range(group)], axis=0)
    _transpose_values(v_ref, vt_ref, seq)
    (l,) = _flash_transposed([q], k_ref, vt_ref, (s_ref,), (p_ref,), (a_ref,), seq, tk)
    o = (a_ref[...] / l).T
    for g in range(group):
        o_ref[:, g * LANES:(g + 1) * LANES] = o[g * tq:(g + 1) * tq].astype(o_ref.dtype)


def _gqa_attention(qr, kr, z, batch, seq, tq=512, tk=512):
    n = z.shape[0]
    nq = seq // tq
    group = GQ_HEADS // GQ_KV_HEADS
    gw = group * LANES
    kern = functools.partial(_gqa_kernel, seq=seq, tq=tq, tk=tk)
    return pl.pallas_call(
        kern,
        grid=(batch, GQ_KV_HEADS, nq),
        in_specs=[
            pl.BlockSpec((tq, gw), lambda b, h, i: (b * nq + i, h)),
            pl.BlockSpec((seq, LANES), lambda b, h, i: (b, h)),
            pl.BlockSpec((seq, LANES), lambda b, h, i: (b, COL_VC + h)),
        ],
        out_specs=pl.BlockSpec((tq, gw), lambda b, h, i: (b * nq + i, h)),
        scratch_shapes=[pltpu.VMEM((LANES, seq), BF16),
                        pltpu.VMEM((2, tk, group * tq), F32), pltpu.VMEM((2, tk, group * tq), BF16),
                        pltpu.VMEM((LANES, group * tq), F32)],
        out_shape=jax.ShapeDtypeStruct((n, GQ_HEADS * LANES), BF16),
        compiler_params=_params("parallel", "parallel", "arbitrary"),
        name="gqa_attention",
    )(qr, kr, z)


def _merge_kernel(h_ref, oa_ref, ob_ref, oc_ref, g0_ref, g1_ref,
                  wa_ref, wb_ref, wc_ref, wo_ref, gp_ref, o_ref):
    d = D_MODEL
    g0 = g0_ref[...]
    g1 = g1_ref[...]
    gate_a = jax.nn.sigmoid(g0[:, :d].astype(F32))
    gate_b = jax.nn.sigmoid(jnp.concatenate([g0[:, d:], g1[:, :2 * d - g0.shape[1]]], axis=1).astype(F32))
    gate_c = jax.nn.sigmoid(g1[:, 2 * d - g0.shape[1]:].astype(F32))
    merged = gate_a * jnp.dot(oa_ref[...], wa_ref[...], preferred_element_type=F32)
    merged += gate_b * jnp.dot(ob_ref[...], wb_ref[...], preferred_element_type=F32)
    merged += gate_c * jnp.dot(oc_ref[...], wc_ref[...], preferred_element_type=F32)
    y = jnp.dot(merged.astype(BF16), wo_ref[...], preferred_element_type=F32)
    o_ref[...] = h_ref[...] + _rms(y, gp_ref[...])


def _merge(h, oa, ob, oc, z, wa, wb, wc, wo, gp, tm=512):
    n, d = h.shape
    gw = 3 * d // 2
    gblk = COL_GZ * LANES // gw
    row = lambda i: (i, 0)
    const = lambda i: (0, 0)
    return pl.pallas_call(
        _merge_kernel,
        grid=(n // tm,),
        in_specs=[
            pl.BlockSpec((tm, d), row),
            pl.BlockSpec((tm, d), row),
            pl.BlockSpec((tm, d), row),
            pl.BlockSpec((tm, d), row),
            pl.BlockSpec((tm, gw), lambda i: (i, gblk)),
            pl.BlockSpec((tm, gw), lambda i: (i, gblk + 1)),
            pl.BlockSpec((d, d), const),
            pl.BlockSpec((d, d), const),
            pl.BlockSpec((d, d), const),
            pl.BlockSpec((d, d), const),
            pl.BlockSpec((1, d), const),
        ],
        out_specs=pl.BlockSpec((tm, d), row),
        out_shape=jax.ShapeDtypeStruct((n, d), F32),
        compiler_params=_params("parallel"),
        name="gated_merge",
    )(h, oa, ob, oc, z, z, wa, wb, wc, wo, gp)


def _mlp_ple_kernel(h_ref, g1_ref, wu_ref, wd_ref, g2_ref, p_ref, wp_ref, wg_ref, g3_ref,
                    o_ref, u_ref, acc_ref):
    j = pl.program_id(1)

    @pl.when(j == 0)
    def _():
        u_ref[...] = _rms(h_ref[...], g1_ref[...]).astype(BF16)
        acc_ref[...] = jnp.zeros_like(acc_ref)

    hid = jnp.dot(u_ref[...], wu_ref[...], preferred_element_type=F32)
    hid = jnp.square(jnp.maximum(hid, 0.0)).astype(BF16)
    acc_ref[...] += jnp.dot(hid, wd_ref[...], preferred_element_type=F32)

    @pl.when(j == pl.num_programs(1) - 1)
    def _():
        h = h_ref[...] + _rms(acc_ref[...], g2_ref[...])
        e = jnp.dot(p_ref[...].astype(BF16), wp_ref[...], preferred_element_type=F32)
        gate = jax.nn.sigmoid(jnp.dot(h.astype(BF16), wg_ref[...], preferred_element_type=F32))
        o_ref[...] = h + _rms(e * gate, g3_ref[...])


def _mlp_ple(h, g1, wu, wd, g2, p, wp, wg, g3, tm=1024, tf=1024):
    n, d = h.shape
    dff = wu.shape[1]
    pd = p.shape[1]
    row = lambda i, j: (i, 0)
    const = lambda i, j: (0, 0)
    return pl.pallas_call(
        _mlp_ple_kernel,
        grid=(n // tm, dff // tf),
        in_specs=[
            pl.BlockSpec((tm, d), row),
            pl.BlockSpec((1, d), const),
            pl.BlockSpec((d, tf), lambda i, j: (0, j)),
            pl.BlockSpec((tf, d), lambda i, j: (j, 0)),
            pl.BlockSpec((1, d), const),
            pl.BlockSpec((tm, pd), row),
            pl.BlockSpec((pd, d), const),
            pl.BlockSpec((d, d), const),
            pl.BlockSpec((1, d), const),
        ],
        out_specs=pl.BlockSpec((tm, d), row),
        out_shape=jax.ShapeDtypeStruct((n, d), F32),
        scratch_shapes=[pltpu.VMEM((tm, d), BF16), pltpu.VMEM((tm, d), F32)],
        compiler_params=_params("parallel", "arbitrary"),
        name="relu2_mlp_ple",
    )(h, g1, wu, wd, g2, p, wp, wg, g3)


def _trunk(x, ple, w):
    batch, seq, d = x.shape
    n = batch * seq
    h = x.reshape(n, d)
    cos, sin = _rope_tables(seq)
    slopes = jnp.exp2(-8.0 * jnp.arange(1, DA_HEADS + 1, dtype=F32) / DA_HEADS)
    col_scale = _in_proj_col_scale()
    for i in range(DEPTH):
        lambda_init = 0.8 - 0.6 * math.exp(-0.3 * i)
        z = _norm_matmul(h, w["g_pre_mix"][i][None], w["w_in"][i], col_scale)
        oa = _diff_attention(z, slopes, w["da_lambda"][i], w["da_norm"][i][None],
                             batch, seq, lambda_init)
        ob = _neigh_attention(z, w["na_tables"][i], batch, seq)
        qr, kr = _qk_rope(z, cos, sin, w["gq_q_norm"][i][None], w["gq_k_norm"][i][None], seq)
        oc = _gqa_attention(qr, kr, z, batch, seq)
        h = _merge(h, oa, ob, oc, z, w["w_br_a"][i], w["w_br_b"][i], w["w_br_c"][i],
                   w["w_o"][i], w["g_post_mix"][i][None])
        h = _mlp_ple(h, w["g_pre_mlp"][i][None], w["w_up"][i], w["w_down"][i],
                     w["g_post_mlp"][i][None], ple[i].reshape(n, PLE_DIM), w["w_ple"][i],
                     w["w_ple_gate"][i], w["g_ple"][i][None])
    return h.reshape(batch, seq, d)


def kernel(x_prompt, x_sample, p_prompt, p_sample, w_in, da_lambda, da_norm, na_rpb, gq_q_norm, gq_k_norm, w_br_a, w_br_b, w_br_c, w_o, g_pre_mix, g_post_mix, g_pre_mlp, g_post_mlp, w_up, w_down, w_ple, w_ple_gate, g_ple):
    w = dict(
        w_in=w_in.astype(BF16), da_lambda=da_lambda, da_norm=da_norm,
        na_tables=jax.vmap(_na_bias_tables)(na_rpb),
        gq_q_norm=gq_q_norm, gq_k_norm=gq_k_norm,
        w_br_a=w_br_a.astype(BF16), w_br_b=w_br_b.astype(BF16), w_br_c=w_br_c.astype(BF16),
        w_o=w_o.astype(BF16), g_pre_mix=g_pre_mix, g_post_mix=g_post_mix,
        g_pre_mlp=g_pre_mlp, g_post_mlp=g_post_mlp,
        w_up=w_up.astype(BF16), w_down=w_down.astype(BF16),
        w_ple=w_ple.astype(BF16), w_ple_gate=w_ple_gate.astype(BF16), g_ple=g_ple,
    )
    return _trunk(x_prompt, p_prompt, w), _trunk(x_sample, p_sample, w)
```

```python
import functools
import math

import jax
import jax.numpy as jnp
from jax import lax
from jax.experimental import pallas as pl
from jax.experimental.pallas import tpu as pltpu

F32 = jnp.float32
BF16 = jnp.bfloat16

D_MODEL = 1024
DEPTH = 4
GRID_W = 64
EPS = 1e-6
DA_HEADS = 8
DA_HEAD_DIM = 64
NA_HEADS = 16
NA_HEAD_DIM = 64
NA_ROWS_MAX = 8
NA_COLS = 16
GQ_HEADS = 8
GQ_KV_HEADS = 2
GQ_HEAD_DIM = 128
ROPE_THETA = 10000.0
D_FF = 4 * D_MODEL
PLE_DIM = 256
IN_W = 10752

LANES = 128
COL_QA, COL_KA, COL_VA = 0, 8, 16
COL_QN, COL_KN, COL_VN = 24, 32, 40
COL_QC, COL_KC, COL_VC = 48, 56, 58
COL_GZ = 60
NEG_BIG = -1e30
LOG2E = math.log2(math.e)

VMEM_LIMIT = 48 * 1024 * 1024


def _params(*sem):
    return pltpu.CompilerParams(dimension_semantics=sem, vmem_limit_bytes=VMEM_LIMIT)


def _rms(x, g):
    return x * lax.rsqrt(jnp.mean(x * x, axis=-1, keepdims=True) + EPS) * g


def _norm_matmul_kernel(x_ref, g_ref, w_ref, cs_ref, o_ref, u_ref):
    @pl.when(pl.program_id(1) == 0)
    def _():
        u_ref[...] = _rms(x_ref[...], g_ref[...]).astype(BF16)

    acc = jnp.dot(u_ref[...], w_ref[...], preferred_element_type=F32)
    o_ref[...] = (acc * cs_ref[...]).astype(o_ref.dtype)


def _in_proj_col_scale():
    cs = jnp.ones((IN_W,), F32)
    cs = cs.at[COL_QA * LANES:COL_KA * LANES].set(DA_HEAD_DIM ** -0.5 * LOG2E)
    cs = cs.at[COL_QN * LANES:COL_KN * LANES].set(NA_HEAD_DIM ** -0.5 * LOG2E)
    return cs[None]


def _norm_matmul(h, g, w, cs, tm=1024, tn=3584):
    n, d = h.shape
    width = w.shape[1]
    return pl.pallas_call(
        _norm_matmul_kernel,
        grid=(n // tm, width // tn),
        in_specs=[
            pl.BlockSpec((tm, d), lambda i, j: (i, 0)),
            pl.BlockSpec((1, d), lambda i, j: (0, 0)),
            pl.BlockSpec((d, tn), lambda i, j: (0, j)),
            pl.BlockSpec((1, tn), lambda i, j: (0, j)),
        ],
        out_specs=pl.BlockSpec((tm, tn), lambda i, j: (i, j)),
        out_shape=jax.ShapeDtypeStruct((n, width), BF16),
        scratch_shapes=[pltpu.VMEM((tm, d), BF16)],
        compiler_params=_params("parallel", "arbitrary"),
        name="norm_in_proj",
    )(h, g, w, cs)


_NT = (((1,), (1,)), ((), ()))
_TN = (((0,), (0,)), ((), ()))


def _transpose_values(v_ref, vt_ref, seq, chunk=512):
    @pl.when(pl.program_id(2) == 0)
    def _():
        for c in range(seq // chunk):
            vt_ref[:, c * chunk:(c + 1) * chunk] = v_ref[c * chunk:(c + 1) * chunk, :].T


def _flash_transposed(q_maps, k_ref, vt_ref, s_refs, p_refs, a_refs, seq, tk,
                      bias_fn=None, shift_fn=None, double=True):
    nmaps = len(q_maps)
    nk = seq // tk

    def start(c):
        return c * tk if isinstance(c, int) else pl.multiple_of(c * tk, tk)

    def shift(c):
        return 0.0 if shift_fn is None else shift_fn(c)

    def scores(kc, slot):
        k = k_ref[pl.ds(start(kc), tk), :]
        bias = None if bias_fn is None else bias_fn(kc)
        mx = []
        for j in range(nmaps):
            s = lax.dot_general(k, q_maps[j], _NT, preferred_element_type=F32)
            if bias is not None:
                s = s - bias
            s_refs[j][slot] = s
            mx.append(jnp.max(s, axis=0, keepdims=True) - shift(kc))
        return tuple(mx)

    def values(i, pslot):
        prev = max(i - 1, 0) if isinstance(i, int) else jnp.maximum(i - 1, 0)
        vt_prev = vt_ref[:, pl.ds(start(prev), tk)]
        return [jnp.dot(vt_prev, p_refs[j][pslot], preferred_element_type=F32) for j in range(nmaps)]

    def softmax(i, slot, carry, pvs):
        ms, ls, mxs = carry
        m_new, l_new = [], []
        for j in range(nmaps):
            if pvs is None:
                mn = mxs[j]
                p = jnp.exp2(s_refs[j][slot] - (mn + shift(i)))
                l_new.append(jnp.sum(p, axis=0, keepdims=True))
                a_refs[j][...] = jnp.zeros_like(a_refs[j])
            else:
                mn = jnp.maximum(ms[j], mxs[j])
                alpha = jnp.exp2(ms[j] - mn)
                p = jnp.exp2(s_refs[j][slot] - (mn + shift(i)))
                l_new.append(alpha * ls[j] + jnp.sum(p, axis=0, keepdims=True))
                a_refs[j][...] = alpha * (a_refs[j][...] + pvs[j])
            m_new.append(mn)
            p_refs[j][slot] = p.astype(BF16)
        return tuple(m_new), tuple(l_new)

    def trip(i, carry, slot=0, first=False, last=False):
        if double:
            mx_next = carry[2] if last else scores(i + 1, 1 - slot)
            m_new, l_new = softmax(i, slot, carry, None if first else values(i, 1 - slot))
        else:
            m_new, l_new = softmax(i, 0, carry, None if first else values(i, 0))
            mx_next = carry[2] if last else scores(i + 1, 0)
        return m_new, l_new, mx_next

    carry = trip(0, (None, None, scores(0, 0)), 0, first=True, last=nk == 1)
    last_slot = 0
    if nk > 1:
        if double:
            done = 1 + 2 * ((nk - 2) // 2)
            carry = lax.fori_loop(
                0, (done - 1) // 2, lambda jj, c: trip(2 * jj + 2, trip(2 * jj + 1, c, 1), 0), carry)
        else:
            done = nk - 1
            carry = lax.fori_loop(1, done, trip, carry)
        for i in range(done, nk - 1):
            carry = trip(i, carry, i % 2)
        last_slot = (nk - 1) % 2 if double else 0
        carry = trip(nk - 1, carry, last_slot, last=True)
    ls = carry[1]
    vt_last = vt_ref[:, (nk - 1) * tk:]
    for j in range(nmaps):
        a_refs[j][...] += jnp.dot(vt_last, p_refs[j][last_slot], preferred_element_type=F32)
    return ls


def _da_kernel(slopes_ref, q_ref, k_ref, v_ref, lam_ref, gn_ref, o_ref,
               vt_ref, bm_ref, s1_ref, s2_ref, p1_ref, p2_ref, a1_ref, a2_ref,
               *, seq, tq, tk, lambda_init, double):
    assert tq == tk
    h = pl.program_id(1)
    q0 = pl.program_id(2) * tq
    slope = slopes_ref[h] * LOG2E
    q = q_ref[...]
    lane = lax.broadcasted_iota(jnp.int32, (tq, LANES), 1)
    zero = jnp.zeros_like(q)
    q1 = jnp.where(lane < DA_HEAD_DIM, q, zero)
    q2 = jnp.where(lane >= DA_HEAD_DIM, q, zero)

    @pl.when(pl.program_id(2) == 0)
    def _():
        srel = slope * (lax.broadcasted_iota(jnp.int32, (tk, tq), 1)
                        - lax.broadcasted_iota(jnp.int32, (tk, tq), 0)).astype(F32)
        bm_ref[0] = srel
        bm_ref[1] = jnp.abs(srel)
        bm_ref[2] = -srel

    def alibi_tile(kc):
        k0 = kc * tk
        return bm_ref[jnp.where(k0 < q0, 0, jnp.where(k0 == q0, 1, 2))]

    def alibi_shift(kc):
        return slope * jnp.abs((q0 - kc * tk).astype(F32))

    _transpose_values(v_ref, vt_ref, seq)
    l1, l2 = _flash_transposed([q1, q2], k_ref, vt_ref, (s1_ref, s2_ref), (p1_ref, p2_ref),
                               (a1_ref, a2_ref), seq, tk, bias_fn=alibi_tile,
                               shift_fn=alibi_shift, double=double)
    lm = lam_ref[...]
    lam = (jnp.exp(jnp.sum(lm[0:1] * lm[1:2], axis=-1, keepdims=True))
           - jnp.exp(jnp.sum(lm[2:3] * lm[3:4], axis=-1, keepdims=True)) + lambda_init)
    o = (a1_ref[...] / l1 - lam * (a2_ref[...] / l2)).T
    o_ref[...] = (_rms(o, gn_ref[...]) * (1.0 - lambda_init)).astype(o_ref.dtype)


DA_MAX_TILE = 1024


def _diff_attention(z, slopes, lam_p, gn, batch, seq, lambda_init, tile=None, double=False):
    n = z.shape[0]
    if tile is None:
        tile = min(DA_MAX_TILE, seq // 4)
    tq = tk = tile
    nq = seq // tq
    kern = functools.partial(_da_kernel, seq=seq, tq=tq, tk=tk, lambda_init=lambda_init,
                             double=double)
    nslot = 2 if double else 1
    return pl.pallas_call(
        kern,
        grid_spec=pltpu.PrefetchScalarGridSpec(
            num_scalar_prefetch=1,
            grid=(batch, DA_HEADS, nq),
            in_specs=[
                pl.BlockSpec((tq, LANES), lambda b, h, i, s: (b * nq + i, COL_QA + h)),
                pl.BlockSpec((seq, LANES), lambda b, h, i, s: (b, COL_KA + h)),
                pl.BlockSpec((seq, LANES), lambda b, h, i, s: (b, COL_VA + h)),
                pl.BlockSpec((4, DA_HEAD_DIM), lambda b, h, i, s: (0, 0)),
                pl.BlockSpec((1, LANES), lambda b, h, i, s: (0, 0)),
            ],
            out_specs=pl.BlockSpec((tq, LANES), lambda b, h, i, s: (b * nq + i, h)),
            scratch_shapes=[pltpu.VMEM((LANES, seq), BF16), pltpu.VMEM((3, tk, tq), F32),
                            pltpu.VMEM((nslot, tk, tq), F32), pltpu.VMEM((nslot, tk, tq), F32),
                            pltpu.VMEM((nslot, tk, tq), BF16), pltpu.VMEM((nslot, tk, tq), BF16),
                            pltpu.VMEM((LANES, tq), F32), pltpu.VMEM((LANES, tq), F32)],
        ),
        out_shape=jax.ShapeDtypeStruct((n, DA_HEADS * LANES), BF16),
        compiler_params=_params("parallel", "parallel", "arbitrary"),
        name="diff_attention",
    )(slopes, z, z, z, lam_p, gn)


NA_SLAB = NA_ROWS_MAX + 1
NA_MASKED = 2 * NA_ROWS_MAX - 1


def _na_kernel(q_ref, k_ref, v_ref, t_ref, o_ref, s_ref, p_ref, l_ref, *, rows):
    npair = rows // 2
    slab = NA_SLAB * GRID_W
    pair_w = 2 * GRID_W
    first = lax.broadcasted_iota(jnp.int32, (GRID_W, LANES), 1) < NA_HEAD_DIM

    def clip(x, lo, hi):
        return min(max(x, lo), hi) if isinstance(x, int) else jnp.clip(x, lo, hi)

    def rows_at(ref, row, nrows):
        off = row * GRID_W
        if not isinstance(off, int):
            off = pl.multiple_of(off, GRID_W)
        return ref.at[pl.ds(off, nrows * GRID_W), :]

    def slab_start(pi):
        return clip(2 * pi - NA_ROWS_MAX // 2, 0, rows - NA_SLAB)

    def scores(pi, slot):
        ws = slab_start(pi)
        kw = rows_at(k_ref, ws, NA_SLAB)[...]
        q = rows_at(q_ref, 2 * pi, 2)[...]
        zero = jnp.zeros_like(q)
        first2 = jnp.concatenate([first, first], axis=0)
        qa = jnp.where(first2, q, zero)
        qb = jnp.where(first2, zero, q)
        q4 = jnp.concatenate([qa[:GRID_W], qb[:GRID_W], qa[GRID_W:], qb[GRID_W:]], axis=0)
        s = lax.dot_general(kw, q4, _NT, preferred_element_type=F32)
        mx = []
        for rq in range(2):
            qrow = 2 * pi + rq
            wq = clip(qrow - NA_ROWS_MAX // 2, 0, rows - NA_ROWS_MAX)
            m = None
            for i in range(NA_SLAB):
                krow = ws + i
                valid = (krow >= wq) & (krow < wq + NA_ROWS_MAX)
                dr = krow - qrow + (NA_ROWS_MAX - 1)
                if isinstance(valid, bool):
                    idx = dr if valid else NA_MASKED
                else:
                    idx = jnp.where(valid, dr, NA_MASKED)
                blk = (s[i * GRID_W:(i + 1) * GRID_W, rq * LANES:(rq + 1) * LANES]
                       + t_ref[0, idx])
                s_ref[slot, i * GRID_W:(i + 1) * GRID_W, rq * LANES:(rq + 1) * LANES] = blk
                bm = jnp.max(blk, axis=0, keepdims=True)
                m = bm if m is None else jnp.maximum(m, bm)
            mx.append(m)
        return jnp.concatenate(mx, axis=1)

    def values(pi, pslot):
        prev = max(pi - 1, 0) if isinstance(pi, int) else jnp.maximum(pi - 1, 0)
        vw = rows_at(v_ref, slab_start(prev), NA_SLAB)[...]
        ot = lax.dot_general(vw, p_ref[pslot], _TN, preferred_element_type=F32) / l_ref[pslot]
        o = ot.T
        out = jnp.concatenate(
            [jnp.where(first, o[:GRID_W], o[GRID_W:pair_w]),
             jnp.where(first, o[pair_w:pair_w + GRID_W], o[pair_w + GRID_W:])], axis=0)
        rows_at(o_ref, 2 * prev, 2)[...] = out.astype(o_ref.dtype)

    def trip(pi, carry, slot=0, last=False):
        mx = carry
        mx_next = mx if last else scores(pi + 1, 1 - slot)
        values(pi, 1 - slot)
        p = jnp.exp2(s_ref[slot] - mx)
        p_ref[slot] = p.astype(BF16)
        l_ref[slot] = jnp.sum(p, axis=0, keepdims=True)
        return mx_next

    p_ref[...] = jnp.zeros_like(p_ref)
    l_ref[...] = jnp.ones_like(l_ref)
    carry = scores(0, 0)
    looped = 2 * ((npair - 1) // 2)
    carry = lax.fori_loop(0, looped // 2,
                          lambda jj, c: trip(2 * jj + 1, trip(2 * jj, c, 0), 1), carry)
    for pi in range(looped, npair - 1):
        carry = trip(pi, carry, pi % 2)
    trip(npair - 1, carry, (npair - 1) % 2, last=True)
    values(npair, (npair - 1) % 2)


def _na_bias_tables(rpb):
    c = jnp.arange(GRID_W)
    col_start = jnp.clip(c - NA_COLS // 2, 0, GRID_W - NA_COLS)
    in_win = (c[None, :] >= col_start[:, None]) & (c[None, :] < col_start[:, None] + NA_COLS)
    dc = jnp.clip(c[None, :] - c[:, None], -(NA_COLS - 1), NA_COLS - 1) + (NA_COLS - 1)
    tm = jnp.where(in_win[None, None], rpb[:, :, dc] * LOG2E, NEG_BIG)
    nd = tm.shape[1]
    tt = tm.reshape(NA_HEADS // 2, 2, nd, GRID_W, GRID_W).transpose(0, 2, 4, 1, 3)
    tt = tt.reshape(NA_HEADS // 2, nd, GRID_W, 2 * GRID_W)
    masked = jnp.full((NA_HEADS // 2, NA_MASKED + 1 - nd, GRID_W, 2 * GRID_W), NEG_BIG, F32)
    return jnp.concatenate([tt, masked], axis=1)


def _neigh_attention(z, tables, batch, seq):
    n = z.shape[0]
    slab = NA_SLAB * GRID_W
    kern = functools.partial(_na_kernel, rows=seq // GRID_W)

    def column(col):
        return pl.BlockSpec((seq, LANES), lambda b, h: (b, col + h))

    return pl.pallas_call(
        kern,
        grid=(batch, NA_HEADS // 2),
        in_specs=[column(COL_QN), column(COL_KN), column(COL_VN),
                  pl.BlockSpec((1,) + tables.shape[1:], lambda b, h: (h, 0, 0, 0))],
        out_specs=column(0),
        out_shape=jax.ShapeDtypeStruct((n, NA_HEADS * NA_HEAD_DIM), BF16),
        scratch_shapes=[pltpu.VMEM((2, slab, 2 * LANES), F32), pltpu.VMEM((2, slab, 2 * LANES), BF16),
                        pltpu.VMEM((2, 1, 2 * LANES), F32)],
        compiler_params=_params("parallel", "parallel"),
        name="neigh_attention",
    )(z, z, z, tables)


def _rope_tables(seq):
    t = jnp.arange(seq)
    row = (t // GRID_W).astype(F32)
    col = (t % GRID_W).astype(F32)
    half = GQ_HEAD_DIM // 2
    freqs = ROPE_THETA ** (-jnp.arange(0, half, 2, dtype=F32) / half)
    ang = jnp.concatenate([row[:, None] * freqs, col[:, None] * freqs], axis=-1)
    cos = jnp.repeat(jnp.cos(ang), 2, axis=-1)
    sin = jnp.repeat(jnp.sin(ang), 2, axis=-1)
    sign = jnp.where(jnp.arange(GQ_HEAD_DIM) % 2 == 0, -1.0, 1.0).astype(F32)
    return cos, sin * sign


def _qk_rope_kernel(q_ref, k_ref, cos_ref, sin_ref, gq_ref, gk_ref, qo_ref, ko_ref):
    cos = cos_ref[...]
    sin = sin_ref[...]
    even = lax.broadcasted_iota(jnp.int32, cos.shape, 1) % 2 == 0

    def rope(x, g):
        y = _rms(x.astype(F32), g)
        swapped = jnp.where(even, pltpu.roll(y, LANES - 1, 1), pltpu.roll(y, 1, 1))
        return y * cos + swapped * sin

    scale = GQ_HEAD_DIM ** -0.5 * LOG2E
    for hd in range(GQ_HEADS):
        sl = slice(hd * LANES, (hd + 1) * LANES)
        qo_ref[:, sl] = (rope(q_ref[:, sl], gq_ref[...]) * scale).astype(qo_ref.dtype)
    for hd in range(GQ_KV_HEADS):
        sl = slice(hd * LANES, (hd + 1) * LANES)
        ko_ref[:, sl] = rope(k_ref[:, sl], gk_ref[...]).astype(ko_ref.dtype)


def _qk_rope(z, cos, sin, gq, gk, seq, tm=512):
    n = z.shape[0]
    nper = seq // tm
    qw = GQ_HEADS * LANES
    kw = GQ_KV_HEADS * LANES
    return pl.pallas_call(
        _qk_rope_kernel,
        grid=(n // tm,),
        in_specs=[
            pl.BlockSpec((tm, qw), lambda i: (i, COL_QC * LANES // qw)),
            pl.BlockSpec((tm, kw), lambda i: (i, COL_KC * LANES // kw)),
            pl.BlockSpec((tm, LANES), lambda i: (i % nper, 0)),
            pl.BlockSpec((tm, LANES), lambda i: (i % nper, 0)),
            pl.BlockSpec((1, LANES), lambda i: (0, 0)),
            pl.BlockSpec((1, LANES), lambda i: (0, 0)),
        ],
        out_specs=[
            pl.BlockSpec((tm, qw), lambda i: (i, 0)),
            pl.BlockSpec((tm, kw), lambda i: (i, 0)),
        ],
        out_shape=[jax.ShapeDtypeStruct((n, qw), BF16), jax.ShapeDtypeStruct((n, kw), BF16)],
        compiler_params=_params("parallel"),
        name="qk_norm_rope",
    )(z, z, cos, sin, gq, gk)


def _gqa_kernel(q_ref, k_ref, v_ref, o_ref, vt_ref, s_ref, p_ref, a_ref, *, seq, tq, tk):
    group = GQ_HEADS // GQ_KV_HEADS
    q = jnp.concatenate([q_ref[:, g * LANES:(g + 1) * LANES] for g in range(group)], axis=0)
    _transpose_values(v_ref, vt_ref, seq)
    (l,) = _flash_transposed([q], k_ref, vt_ref, (s_ref,), (p_ref,), (a_ref,), seq, tk)
    o = (a_ref[...] / l).T
    for g in range(group):
        o_ref[:, g * LANES:(g + 1) * LANES] = o[g * tq:(g + 1) * tq].astype(o_ref.dtype)


def _gqa_attention(qr, kr, z, batch, seq, tq=512, tk=512):
    n = z.shape[0]
    nq = seq // tq
    group = GQ_HEADS // GQ_KV_HEADS
    gw = group * LANES
    kern = functools.partial(_gqa_kernel, seq=seq, tq=tq, tk=tk)
    return pl.pallas_call(
        kern,
        grid=(batch, GQ_KV_HEADS, nq),
        in_specs=[
            pl.BlockSpec((tq, gw), lambda b, h, i: (b * nq + i, h)),
            pl.BlockSpec((seq, LANES), lambda b, h, i: (b, h)),
            pl.BlockSpec((seq, LANES), lambda b, h, i: (b, COL_VC + h)),
        ],
        out_specs=pl.BlockSpec((tq, gw), lambda b, h, i: (b * nq + i, h)),
        scratch_shapes=[pltpu.VMEM((LANES, seq), BF16),
                        pltpu.VMEM((2, tk, group * tq), F32), pltpu.VMEM((2, tk, group * tq), BF16),
                        pltpu.VMEM((LANES, group * tq), F32)],
        out_shape=jax.ShapeDtypeStruct((n, GQ_HEADS * LANES), BF16),
        compiler_params=_params("parallel", "parallel", "arbitrary"),
        name="gqa_attention",
    )(qr, kr, z)


def _merge_kernel(h_ref, oa_ref, ob_ref, oc_ref, g0_ref, g1_ref,
                  wa_ref, wb_ref, wc_ref, wo_ref, gp_ref, o_ref):
    d = D_MODEL
    g0 = g0_ref[...]
    g1 = g1_ref[...]
    gate_a = jax.nn.sigmoid(g0[:, :d].astype(F32))
    gate_b = jax.nn.sigmoid(jnp.concatenate([g0[:, d:], g1[:, :2 * d - g0.shape[1]]], axis=1).astype(F32))
    gate_c = jax.nn.sigmoid(g1[:, 2 * d - g0.shape[1]:].astype(F32))
    merged = gate_a * jnp.dot(oa_ref[...], wa_ref[...], preferred_element_type=F32)
    merged += gate_b * jnp.dot(ob_ref[...], wb_ref[...], preferred_element_type=F32)
    merged += gate_c * jnp.dot(oc_ref[...], wc_ref[...], preferred_element_type=F32)
    y = jnp.dot(merged.astype(BF16), wo_ref[...], preferred_element_type=F32)
    o_ref[...] = h_ref[...] + _rms(y, gp_ref[...])


def _merge(h, oa, ob, oc, z, wa, wb, wc, wo, gp, tm=512):
    n, d = h.shape
    gw = 3 * d // 2
    gblk = COL_GZ * LANES // gw
    row = lambda i: (i, 0)
    const = lambda i: (0, 0)
    return pl.pallas_call(
        _merge_kernel,
        grid=(n // tm,),
        in_specs=[
            pl.BlockSpec((tm, d), row),
            pl.BlockSpec((tm, d), row),
            pl.BlockSpec((tm, d), row),
            pl.BlockSpec((tm, d), row),
            pl.BlockSpec((tm, gw), lambda i: (i, gblk)),
            pl.BlockSpec((tm, gw), lambda i: (i, gblk + 1)),
            pl.BlockSpec((d, d), const),
            pl.BlockSpec((d, d), const),
            pl.BlockSpec((d, d), const),
            pl.BlockSpec((d, d), const),
            pl.BlockSpec((1, d), const),
        ],
        out_specs=pl.BlockSpec((tm, d), row),
        out_shape=jax.ShapeDtypeStruct((n, d), F32),
        compiler_params=_params("parallel"),
        name="gated_merge",
    )(h, oa, ob, oc, z, z, wa, wb, wc, wo, gp)


def _mlp_ple_kernel(h_ref, g1_ref, wu_ref, wd_ref, g2_ref, p_ref, wp_ref, wg_ref, g3_ref,
                    o_ref, u_ref, acc_ref):
    j = pl.program_id(1)

    @pl.when(j == 0)
    def _():
        u_ref[...] = _rms(h_ref[...], g1_ref[...]).astype(BF16)
        acc_ref[...] = jnp.zeros_like(acc_ref)

    hid = jnp.dot(u_ref[...], wu_ref[...], preferred_element_type=F32)
    hid = jnp.square(jnp.maximum(hid, 0.0)).astype(BF16)
    acc_ref[...] += jnp.dot(hid, wd_ref[...], preferred_element_type=F32)

    @pl.when(j == pl.num_programs(1) - 1)
    def _():
        h = h_ref[...] + _rms(acc_ref[...], g2_ref[...])
        e = jnp.dot(p_ref[...].astype(BF16), wp_ref[...], preferred_element_type=F32)
        gate = jax.nn.sigmoid(jnp.dot(h.astype(BF16), wg_ref[...], preferred_element_type=F32))
        o_ref[...] = h + _rms(e * gate, g3_ref[...])


def _mlp_ple(h, g1, wu, wd, g2, p, wp, wg, g3, tm=1024, tf=1024):
    n, d = h.shape
    dff = wu.shape[1]
    pd = p.shape[1]
    row = lambda i, j: (i, 0)
    const = lambda i, j: (0, 0)
    return pl.pallas_call(
        _mlp_ple_kernel,
        grid=(n // tm, dff // tf),
        in_specs=[
            pl.BlockSpec((tm, d), row),
            pl.BlockSpec((1, d), const),
            pl.BlockSpec((d, tf), lambda i, j: (0, j)),
            pl.BlockSpec((tf, d), lambda i, j: (j, 0)),
            pl.BlockSpec((1, d), const),
            pl.BlockSpec((tm, pd), row),
            pl.BlockSpec((pd, d), const),
            pl.BlockSpec((d, d), const),
            pl.BlockSpec((1, d), const),
        ],
        out_specs=pl.BlockSpec((tm, d), row),
        out_shape=jax.ShapeDtypeStruct((n, d), F32),
        scratch_shapes=[pltpu.VMEM((tm, d), BF16), pltpu.VMEM((tm, d), F32)],
        compiler_params=_params("parallel", "arbitrary"),
        name="relu2_mlp_ple",
    )(h, g1, wu, wd, g2, p, wp, wg, g3)


def _trunk(x, ple, w):
    batch, seq, d = x.shape
    n = batch * seq
    h = x.reshape(n, d)
    cos, sin = _rope_tables(seq)
    slopes = jnp.exp2(-8.0 * jnp.arange(1, DA_HEADS + 1, dtype=F32) / DA_HEADS)
    col_scale = _in_proj_col_scale()
    for i in range(DEPTH):
        lambda_init = 0.8 - 0.6 * math.exp(-0.3 * i)
        z = _norm_matmul(h, w["g_pre_mix"][i][None], w["w_in"][i], col_scale)
        oa = _diff_attention(z, slopes, w["da_lambda"][i], w["da_norm"][i][None],
                             batch, seq, lambda_init)
        ob = _neigh_attention(z, w["na_tables"][i], batch, seq)
        qr, kr = _qk_rope(z, cos, sin, w["gq_q_norm"][i][None], w["gq_k_norm"][i][None], seq)
        oc = _gqa_attention(qr, kr, z, batch, seq)
        h = _merge(h, oa, ob, oc, z, w["w_br_a"][i], w["w_br_b"][i], w["w_br_c"][i],
                   w["w_o"][i], w["g_post_mix"][i][None])
        h = _mlp_ple(h, w["g_pre_mlp"][i][None], w["w_up"][i], w["w_down"][i],
                     w["g_post_mlp"][i][None], ple[i].reshape(n, PLE_DIM), w["w_ple"][i],
                     w["w_ple_gate"][i], w["g_ple"][i][None])
    return h.reshape(batch, seq, d)


def kernel(x_prompt, x_sample, p_prompt, p_sample, w_in, da_lambda, da_norm, na_rpb, gq_q_norm, gq_k_norm, w_br_a, w_br_b, w_br_c, w_o, g_pre_mix, g_post_mix, g_pre_mlp, g_post_mlp, w_up, w_down, w_ple, w_ple_gate, g_ple):
    w = dict(
        w_in=w_in.astype(BF16), da_lambda=da_lambda, da_norm=da_norm,
        na_tables=jax.vmap(_na_bias_tables)(na_rpb),
        gq_q_norm=gq_q_norm, gq_k_norm=gq_k_norm,
        w_br_a=w_br_a.astype(BF16), w_br_b=w_br_b.astype(BF16), w_br_c=w_br_c.astype(BF16),
        w_o=w_o.astype(BF16), g_pre_mix=g_pre_mix, g_post_mix=g_post_mix,
        g_pre_mlp=g_pre_mlp, g_post_mlp=g_post_mlp,
        w_up=w_up.astype(BF16), w_down=w_down.astype(BF16),
        w_ple=w_ple.astype(BF16), w_ple_gate=w_ple_gate.astype(BF16), g_ple=g_ple,
    )
    return _trunk(x_prompt, p_prompt, w), _trunk(x_sample, p_sample, w)
```
